```python
import jax, jax.numpy as jnp
from jax import lax
import numpy as np

D_MODEL = 1024
BATCH = 8
SEQ = 4096
DEPTH = 4
DEC_BATCH = 16
DEC_SEQ = 2048
PAST_LEN = 128

N_MIXERS = 4
GRID_W = 64
BLOCK = 128
A_HALF = 2 * D_MODEL
A_GROUPS = 8
A_GDIM = A_HALF // A_GROUPS
HEAD_DIM = 64
N_Q_HEADS = D_MODEL // HEAD_DIM
N_KV_HEADS = 4
Q_PER_KV = N_Q_HEADS // N_KV_HEADS
ROPE_THETA = 10000.0
POOL_WINDOWS = (2, 4, 8, 16)
C_GROUPS = len(POOL_WINDOWS)
C_GDIM = D_MODEL // C_GROUPS
CONV_WIDTH = 31
D_FF = 3584
N_EXPERTS = 8
TOP_K = 2
DN_ALPHA = (2.0 * DEPTH) ** 0.25
DN_BETA = (8.0 * DEPTH) ** -0.25
LN_EPS = 1e-5
RMS_EPS = 1e-6
N_A = (DEPTH + 3) // 4
N_B = (DEPTH + 2) // 4
N_C = (DEPTH + 1) // 4
N_D = DEPTH // 4
N_DENSE = (DEPTH + 1) // 2
N_MOE = DEPTH // 2

kernel_name = "hybrid_bidir_encoder_interleaved"


def _layernorm(x, g, b):
    xf = x.astype(jnp.float32)
    mu = jnp.mean(xf, -1, keepdims=True)
    xc = xf - mu
    var = jnp.mean(xc * xc, -1, keepdims=True)
    return (xc * lax.rsqrt(var + LN_EPS) * g.astype(jnp.float32) + b.astype(jnp.float32)).astype(x.dtype)


def _rmsnorm(x, g):
    xf = x.astype(jnp.float32)
    return (xf * lax.rsqrt(jnp.mean(xf * xf, -1, keepdims=True) + RMS_EPS) * g.astype(jnp.float32)).astype(x.dtype)


def _mixer_gmlp(x, w_in, b_in, ln_g, ln_b, w_s, b_s, w_out):
    bsz, n, _ = x.shape
    z = jax.nn.gelu(x @ w_in + b_in)
    u, v = jnp.split(z, 2, axis=-1)
    v = _layernorm(v, ln_g, ln_b)
    v = v.reshape(bsz, n // BLOCK, BLOCK, A_GROUPS, A_GDIM)
    sv = jnp.einsum('gpq,bcqgd->bcpgd', w_s, v) + b_s.T[:, :, None]
    return (u * sv.reshape(bsz, n, A_HALF)) @ w_out


def _axial_rope_tables(n):
    rows = n // GRID_W
    row = jnp.repeat(jnp.arange(rows), GRID_W).astype(jnp.float32)
    col = jnp.tile(jnp.arange(GRID_W), rows).astype(jnp.float32)
    sec = HEAD_DIM // 2
    inv = ROPE_THETA ** (-jnp.arange(0, sec, 2, dtype=jnp.float32) / sec)
    ang_r = row[:, None] * inv[None, :]
    ang_c = col[:, None] * inv[None, :]
    ang = jnp.concatenate([ang_r, ang_r, ang_c, ang_c], axis=-1)
    return jnp.cos(ang), jnp.sin(ang)


def _rotate_half_sections(x):
    xs = x.reshape(x.shape[:-1] + (2, 2, HEAD_DIM // 4))
    return jnp.concatenate([-xs[..., 1:, :], xs[..., :1, :]], axis=-2).reshape(x.shape)


def _apply_rope(x, cos, sin):
    xf = x.astype(jnp.float32)
    return (xf * cos + _rotate_half_sections(xf) * sin).astype(x.dtype)


def _mixer_attention(x, w_qkv, q_g, k_g, w_o):
    bsz, n, _ = x.shape
    qkv = x @ w_qkv
    q, k, v = jnp.split(qkv, [N_Q_HEADS * HEAD_DIM, (N_Q_HEADS + N_KV_HEADS) * HEAD_DIM], axis=-1)
    q = q.reshape(bsz, n, N_Q_HEADS, HEAD_DIM)
    k = k.reshape(bsz, n, N_KV_HEADS, HEAD_DIM)
    v = v.reshape(bsz, n, N_KV_HEADS, HEAD_DIM)
    cos, sin = _axial_rope_tables(n)
    cos = cos[:, None, :]
    sin = sin[:, None, :]
    q = _apply_rope(_rmsnorm(q, q_g), cos, sin)
    k = _apply_rope(_rmsnorm(k, k_g), cos, sin)
    nblk = n // BLOCK
    qb = q.reshape(bsz, nblk, BLOCK, N_KV_HEADS, Q_PER_KV, HEAD_DIM).transpose(1, 0, 2, 3, 4, 5)
    scale = HEAD_DIM ** -0.5

    def attend_block(qblk):
        s = jnp.einsum('bqkgd,bskd->bkgqs', qblk, k, preferred_element_type=jnp.float32) * scale
        p = jax.nn.softmax(s, axis=-1).astype(v.dtype)
        return jnp.einsum('bkgqs,bskd->bqkgd', p, v)

    o = lax.map(attend_block, qb)
    o = o.transpose(1, 0, 2, 3, 4, 5).reshape(bsz, n, N_Q_HEADS * HEAD_DIM)
    return o @ w_o


def _mixer_pool(x, w_g, b, scale):
    bsz, n, _ = x.shape
    xf = x.astype(jnp.float32)
    csum = jnp.concatenate([jnp.zeros((bsz, 1, D_MODEL), jnp.float32), jnp.cumsum(xf, axis=1)], axis=1)
    t = jnp.arange(n)
    pooled = []
    for gi, w in enumerate(POOL_WINDOWS):
        lo = jnp.clip(t - w // 2, 0, n)
        hi = jnp.clip(t + w - w // 2, 0, n)
        cg = csum[..., gi * C_GDIM:(gi + 1) * C_GDIM]
        wsum = jnp.take(cg, hi, axis=1) - jnp.take(cg, lo, axis=1)
        cnt = (hi - lo).astype(jnp.float32)[None, :, None]
        pooled.append(wsum / cnt)
    mix = (jnp.concatenate(pooled, axis=-1) - xf).astype(x.dtype)
    mix = mix.reshape(bsz, n, C_GROUPS, C_GDIM)
    y = jnp.einsum('bngc,gce->bnge', mix, w_g).reshape(bsz, n, D_MODEL) + b
    return y * scale


def _mixer_conv(x, w_in, b_in, w_dw, b_dw, ln_g, ln_b, w_out, b_out):
    h = x @ w_in + b_in
    a, g = jnp.split(h, 2, axis=-1)
    h = a * jax.nn.sigmoid(g)
    h = lax.conv_general_dilated(h, w_dw[:, None, :], window_strides=(1,),
                                 padding=[(CONV_WIDTH // 2, CONV_WIDTH // 2)],
                                 dimension_numbers=('NWC', 'WIO', 'NWC'),
                                 feature_group_count=D_MODEL) + b_dw
    h = jax.nn.silu(_layernorm(h, ln_g, ln_b))
    return h @ w_out + b_out


def _swiglu(x, w1, w3, w2):
    return (jax.nn.silu(x @ w1) * (x @ w3)) @ w2


def _moe(x, w_r, b_r, w1, w3, w2):
    bsz, n, d = x.shape
    xt = x.reshape(bsz * n, d)
    logits = (xt @ w_r + b_r).astype(jnp.float32)
    top_v, top_i = lax.top_k(logits, TOP_K)
    gates = jax.nn.softmax(top_v, axis=-1)
    combine = jnp.einsum('tk,tke->te', gates, jax.nn.one_hot(top_i, N_EXPERTS, dtype=jnp.float32)).astype(x.dtype)
    y = jnp.zeros_like(xt)
    for e in range(N_EXPERTS):
        y = y + combine[:, e:e + 1] * _swiglu(xt, w1[e], w3[e], w2[e])
    return y.reshape(bsz, n, d)


def _trunk(x, ln_mix_g, ln_mix_b, ln_ffn_g, ln_ffn_b,
           a_w_in, a_b_in, a_ln_g, a_ln_b, a_w_s, a_b_s, a_w_out,
           b_w_qkv, b_q_g, b_k_g, b_w_o,
           c_w, c_b, c_scale,
           d_w_in, d_b_in, d_w_dw, d_b_dw, d_ln_g, d_ln_b, d_w_out, d_b_out,
           f_w1, f_w3, f_w2,
           m_w_r, m_b_r, m_w1, m_w3, m_w2):
    for l in range(DEPTH):
        kind = l % N_MIXERS
        i = l // N_MIXERS
        if kind == 0:
            h = _mixer_gmlp(x, a_w_in[i], a_b_in[i], a_ln_g[i], a_ln_b[i], a_w_s[i], a_b_s[i], a_w_out[i])
        elif kind == 1:
            h = _mixer_attention(x, b_w_qkv[i], b_q_g[i], b_k_g[i], b_w_o[i])
        elif kind == 2:
            h = _mixer_pool(x, c_w[i], c_b[i], c_scale[i])
        else:
            h = _mixer_conv(x, d_w_in[i], d_b_in[i], d_w_dw[i], d_b_dw[i], d_ln_g[i], d_ln_b[i], d_w_out[i], d_b_out[i])
        x = _layernorm(DN_ALPHA * x + h, ln_mix_g[l], ln_mix_b[l])
        j = l // 2
        if l % 2 == 0:
            h = _swiglu(x, f_w1[j], f_w3[j], f_w2[j])
        else:
            h = _moe(x, m_w_r[j], m_b_r[j], m_w1[j], m_w3[j], m_w2[j])
        x = _layernorm(DN_ALPHA * x + h, ln_ffn_g[l], ln_ffn_b[l])
    return x


def setup_inputs(seed: int = 0) -> dict:
    key = jax.random.key(seed)
    ks = iter(jax.random.split(key, 48))

    def nrm(shape, scale):
        return jax.random.normal(next(ks), shape, jnp.float32) * scale

    D = D_MODEL
    QKV = (N_Q_HEADS + 2 * N_KV_HEADS) * HEAD_DIM
    return {
        "x_prompt": nrm((BATCH, SEQ, D), 1.0),
        "x_sample": nrm((DEC_BATCH, DEC_SEQ, D), 1.0),
        "ln_mix_g": 1.0 + nrm((DEPTH, D), 0.02),
        "ln_mix_b": nrm((DEPTH, D), 0.02),
        "ln_ffn_g": 1.0 + nrm((DEPTH, D), 0.02),
        "ln_ffn_b": nrm((DEPTH, D), 0.02),
        "a_w_in": nrm((N_A, D, 2 * A_HALF), D ** -0.5),
        "a_b_in": nrm((N_A, 2 * A_HALF), 0.02),
        "a_ln_g": 1.0 + nrm((N_A, A_HALF), 0.02),
        "a_ln_b": nrm((N_A, A_HALF), 0.02),
        "a_w_s": nrm((N_A, A_GROUPS, BLOCK, BLOCK), BLOCK ** -0.5),
        "a_b_s": 1.0 + nrm((N_A, A_GROUPS, BLOCK), 0.02),
        "a_w_out": nrm((N_A, A_HALF, D), A_HALF ** -0.5 * DN_BETA),
        "b_w_qkv": nrm((N_B, D, QKV), D ** -0.5),
        "b_q_g": 1.0 + nrm((N_B, HEAD_DIM), 0.02),
        "b_k_g": 1.0 + nrm((N_B, HEAD_DIM), 0.02),
        "b_w_o": nrm((N_B, N_Q_HEADS * HEAD_DIM, D), (N_Q_HEADS * HEAD_DIM) ** -0.5 * DN_BETA),
        "c_w": nrm((N_C, C_GROUPS, C_GDIM, C_GDIM), C_GDIM ** -0.5 * DN_BETA),
        "c_b": nrm((N_C, D), 0.02),
        "c_scale": 1.0 + nrm((N_C, D), 0.02),
        "d_w_in": nrm((N_D, D, 2 * D), D ** -0.5),
        "d_b_in": nrm((N_D, 2 * D), 0.02),
        "d_w_dw": nrm((N_D, CONV_WIDTH, D), CONV_WIDTH ** -0.5),
        "d_b_dw": nrm((N_D, D), 0.02),
        "d_ln_g": 1.0 + nrm((N_D, D), 0.02),
        "d_ln_b": nrm((N_D, D), 0.02),
        "d_w_out": nrm((N_D, D, D), D ** -0.5 * DN_BETA),
        "d_b_out": nrm((N_D, D), 0.02),
        "f_w1": nrm((N_DENSE, D, D_FF), D ** -0.5),
        "f_w3": nrm((N_DENSE, D, D_FF), D ** -0.5),
        "f_w2": nrm((N_DENSE, D_FF, D), D_FF ** -0.5 * DN_BETA),
        "m_w_r": nrm((N_MOE, D, N_EXPERTS), D ** -0.5),
        "m_b_r": nrm((N_MOE, N_EXPERTS), 0.01),
        "m_w1": nrm((N_MOE, N_EXPERTS, D, D_FF), D ** -0.5),
        "m_w3": nrm((N_MOE, N_EXPERTS, D, D_FF), D ** -0.5),
        "m_w2": nrm((N_MOE, N_EXPERTS, D_FF, D), D_FF ** -0.5 * DN_BETA),
    }


def reference(x_prompt, x_sample, ln_mix_g, ln_mix_b, ln_ffn_g, ln_ffn_b,
              a_w_in, a_b_in, a_ln_g, a_ln_b, a_w_s, a_b_s, a_w_out,
              b_w_qkv, b_q_g, b_k_g, b_w_o,
              c_w, c_b, c_scale,
              d_w_in, d_b_in, d_w_dw, d_b_dw, d_ln_g, d_ln_b, d_w_out, d_b_out,
              f_w1, f_w3, f_w2,
              m_w_r, m_b_r, m_w1, m_w3, m_w2):
    weights = (ln_mix_g, ln_mix_b, ln_ffn_g, ln_ffn_b,
               a_w_in, a_b_in, a_ln_g, a_ln_b, a_w_s, a_b_s, a_w_out,
               b_w_qkv, b_q_g, b_k_g, b_w_o,
               c_w, c_b, c_scale,
               d_w_in, d_b_in, d_w_dw, d_b_dw, d_ln_g, d_ln_b, d_w_out, d_b_out,
               f_w1, f_w3, f_w2,
               m_w_r, m_b_r, m_w1, m_w3, m_w2)
    y_prompt = _trunk(x_prompt, *weights)
    y_sample = _trunk(x_sample, *weights)
    return (y_prompt, y_sample)
```

```python
import functools
import math

import jax
import jax.numpy as jnp
from jax import lax
from jax.experimental import pallas as pl
from jax.experimental.pallas import tpu as pltpu

F32 = jnp.float32
BF16 = jnp.bfloat16

GRID_W = 64
BLOCK = 128
A_GROUPS = 8
HEAD_DIM = 64
N_KV_HEADS = 4
ROPE_THETA = 10000.0
POOL_WINDOWS = (2, 4, 8, 16)
TOP_K = 2
LN_EPS = 1e-5
RMS_EPS = 1e-6

LANES = 128
SUBLANES_F32 = 8
SUBLANES_BF16 = 16
VMEM_LIMIT = 56 * 1024 * 1024


def _params(*sem):
    return pltpu.CompilerParams(dimension_semantics=sem, vmem_limit_bytes=VMEM_LIMIT)


def _tile(total, target):
    t = math.gcd(total, target)
    assert t % SUBLANES_BF16 == 0, (total, target)
    return t


def _ln_rows(y, g, b):
    mu = jnp.mean(y, axis=-1, keepdims=True)
    yc = y - mu
    var = jnp.mean(yc * yc, axis=-1, keepdims=True)
    return yc * lax.rsqrt(var + LN_EPS) * g + b


def _gelu_tanh(x):
    c = math.sqrt(2.0 / math.pi)
    return x * (0.5 * (1.0 + jnp.tanh(c * (x + 0.044715 * (x * x * x)))))


def _silu(x):
    return x * jax.nn.sigmoid(x)


def _seq_pos(t0, t1, n1, n2):
    in_g1 = t0 < t1
    n = jnp.where(in_g1, n1, n2)
    base = jnp.where(in_g1, 0, t1)
    return n, lax.rem(t0 - base, n)


def _store_res_ln(y, g_ref, b_ref, xo_ref, xb_ref):
    out = _ln_rows(y, g_ref[...], b_ref[...])
    xo_ref[...] = out
    xb_ref[...] = out.astype(BF16)


def _mm_gelu_kernel(x_ref, w_ref, b_ref, o_ref):
    z = jnp.dot(x_ref[...], w_ref[...], preferred_element_type=F32) + b_ref[...]
    o_ref[...] = _gelu_tanh(z).astype(o_ref.dtype)


def _mm_gelu(xb, w, b):
    t, k = xb.shape
    n = w.shape[1]
    tm, tn = _tile(t, 1024), _tile(n, 1024)
    return pl.pallas_call(
        _mm_gelu_kernel,
        grid=(n // tn, t // tm),
        in_specs=[pl.BlockSpec((tm, k), lambda j, i: (i, 0)),
                  pl.BlockSpec((k, tn), lambda j, i: (0, j)),
                  pl.BlockSpec((1, tn), lambda j, i: (0, j))],
        out_specs=pl.BlockSpec((tm, tn), lambda j, i: (i, j)),
        out_shape=jax.ShapeDtypeStruct((t, n), BF16),
        compiler_params=_params("parallel", "parallel"),
        name="mm_gelu",
    )(xb, w, b)


def _mm_glu_kernel(x_ref, wa_ref, wg_ref, ba_ref, bg_ref, o_ref):
    x = x_ref[...]
    a = jnp.dot(x, wa_ref[...], preferred_element_type=F32) + ba_ref[...]
    g = jnp.dot(x, wg_ref[...], preferred_element_type=F32) + bg_ref[...]
    o_ref[...] = (a * jax.nn.sigmoid(g)).astype(o_ref.dtype)


def _mm_glu(xb, w, b):
    t, k = xb.shape
    n = w.shape[1] // 2
    tm, tn = _tile(t, 1024), _tile(n, 512)
    nj = n // tn
    return pl.pallas_call(
        _mm_glu_kernel,
        grid=(nj, t // tm),
        in_specs=[pl.BlockSpec((tm, k), lambda j, i: (i, 0)),
                  pl.BlockSpec((k, tn), lambda j, i: (0, j)),
                  pl.BlockSpec((k, tn), lambda j, i: (0, j + nj)),
                  pl.BlockSpec((1, tn), lambda j, i: (0, j)),
                  pl.BlockSpec((1, tn), lambda j, i: (0, j + nj))],
        out_specs=pl.BlockSpec((tm, tn), lambda j, i: (i, j)),
        out_shape=jax.ShapeDtypeStruct((t, n), BF16),
        compiler_params=_params("parallel", "parallel"),
        name="mm_glu",
    )(xb, w, w, b, b)


def _gmlp_gate_kernel(z_ref, x_ref, vg_ref, vb_ref, ws_ref, bs_ref, wo_ref, g_ref, b_ref,
                      xo_ref, xb_ref, gated_ref, *, alpha, half, gdim):
    tm = z_ref.shape[0]
    v = _ln_rows(z_ref[:, half:].astype(F32), vg_ref[...], vb_ref[...]).astype(BF16)
    for c in range(tm // BLOCK):
        rows = slice(c * BLOCK, (c + 1) * BLOCK)
        for gi in range(A_GROUPS):
            cols = slice(gi * gdim, (gi + 1) * gdim)
            sv = jnp.dot(ws_ref[gi], v[rows, cols], preferred_element_type=F32) + bs_ref[:, gi:gi + 1]
            gated_ref[rows, cols] = (z_ref[rows, cols].astype(F32) * sv).astype(BF16)
    h = jnp.dot(gated_ref[...], wo_ref[...], preferred_element_type=F32)
    _store_res_ln(alpha * x_ref[...] + h, g_ref, b_ref, xo_ref, xb_ref)


def _gmlp_gate(z, x, v_g, v_b, w_s, b_s_t, w_out, ln_g, ln_b, alpha):
    t, d = x.shape
    half = z.shape[1] // 2
    tm = _tile(t, 2 * BLOCK)
    row = lambda i: (i, 0)
    fixed2 = lambda i: (0, 0)
    return pl.pallas_call(
        functools.partial(_gmlp_gate_kernel, alpha=alpha, half=half, gdim=half // A_GROUPS),
        grid=(t // tm,),
        in_specs=[pl.BlockSpec((tm, 2 * half), row),
                  pl.BlockSpec((tm, d), row),
                  pl.BlockSpec((1, half), fixed2),
                  pl.BlockSpec((1, half), fixed2),
                  pl.BlockSpec((A_GROUPS, BLOCK, BLOCK), lambda i: (0, 0, 0)),
                  pl.BlockSpec((BLOCK, A_GROUPS), fixed2),
                  pl.BlockSpec((half, d), fixed2),
                  pl.BlockSpec((1, d), fixed2),
                  pl.BlockSpec((1, d), fixed2)],
        out_specs=[pl.BlockSpec((tm, d), row), pl.BlockSpec((tm, d), row)],
        out_shape=[jax.ShapeDtypeStruct((t, d), F32), jax.ShapeDtypeStruct((t, d), BF16)],
        scratch_shapes=[pltpu.VMEM((tm, half), BF16)],
        compiler_params=_params("parallel"),
        name="gmlp_gate",
    )(z, x, v_g, v_b, w_s, b_s_t, w_out, ln_g, ln_b)


def _ffn_accumulate(i, nv_ref, x_ref, w1_ref, w3_ref, w2_ref, acc_ref):
    c = pl.program_id(1)

    @pl.when(c == 0)
    def _():
        acc_ref[...] = jnp.zeros_like(acc_ref)

    @pl.when(i < nv_ref[0])
    def _():
        x = x_ref[...]
        a = jnp.dot(x, w1_ref[0], preferred_element_type=F32)
        b = jnp.dot(x, w3_ref[0], preferred_element_type=F32)
        mid = (_silu(a) * b).astype(BF16)
        acc_ref[...] += jnp.dot(mid, w2_ref[0], preferred_element_type=F32)


def _ffn_dense_kernel(te_ref, nv_ref, x_ref, w1_ref, w3_ref, w2_ref, r_ref, g_ref, b_ref,
                      xo_ref, xb_ref, acc_ref, *, alpha):
    _ffn_accumulate(pl.program_id(0), nv_ref, x_ref, w1_ref, w3_ref, w2_ref, acc_ref)

    @pl.when(pl.program_id(1) == pl.num_programs(1) - 1)
    def _():
        _store_res_ln(alpha * r_ref[...] + acc_ref[...], g_ref, b_ref, xo_ref, xb_ref)


def _ffn_moe_kernel(te_ref, nv_ref, x_ref, w1_ref, w3_ref, w2_ref, gate_ref, o_ref, acc_ref):
    _ffn_accumulate(pl.program_id(0), nv_ref, x_ref, w1_ref, w3_ref, w2_ref, acc_ref)

    @pl.when(pl.program_id(1) == pl.num_programs(1) - 1)
    def _():
        o_ref[...] = (acc_ref[...] * gate_ref[...]).astype(o_ref.dtype)


def _ffn_specs(tm, d, f, tf):
    nf = f // tf

    def chunk(i, c, te, nv):
        return jnp.where(i < nv[0], c, nf - 1)

    def row(i, c, te, nv):
        return (jnp.minimum(i, nv[0] - 1), 0)

    return [pl.BlockSpec((tm, d), row),
            pl.BlockSpec((1, d, tf), lambda i, c, te, nv: (te[i], 0, chunk(i, c, te, nv))),
            pl.BlockSpec((1, d, tf), lambda i, c, te, nv: (te[i], 0, chunk(i, c, te, nv))),
            pl.BlockSpec((1, tf, d), lambda i, c, te, nv: (te[i], chunk(i, c, te, nv), 0))], nf


def _ffn_dense(xb, x, w1, w3, w2, ln_g, ln_b, alpha):
    t, d = x.shape
    f = w1.shape[2]
    tm, tf = _tile(t, 1024), _tile(f, 512)
    specs, nf = _ffn_specs(tm, d, f, tf)
    row = lambda i, c, te, nv: (i, 0)
    fixed = lambda i, c, te, nv: (0, 0)
    te = jnp.zeros((t // tm,), jnp.int32)
    nv = jnp.full((1,), t // tm, jnp.int32)
    return pl.pallas_call(
        functools.partial(_ffn_dense_kernel, alpha=alpha),
        grid_spec=pltpu.PrefetchScalarGridSpec(
            num_scalar_prefetch=2, grid=(t // tm, nf),
            in_specs=specs + [pl.BlockSpec((tm, d), row), pl.BlockSpec((1, d), fixed), pl.BlockSpec((1, d), fixed)],
            out_specs=[pl.BlockSpec((tm, d), row), pl.BlockSpec((tm, d), row)],
            scratch_shapes=[pltpu.VMEM((tm, d), F32)]),
        out_shape=[jax.ShapeDtypeStruct((t, d), F32), jax.ShapeDtypeStruct((t, d), BF16)],
        compiler_params=_params("parallel", "arbitrary"),
        name="ffn_dense",
    )(te, nv, xb, w1, w3, w2, x, ln_g, ln_b)


def _ffn_moe(xs, gate_p, te, nv, w1, w3, w2, tm):
    p, d = xs.shape
    f = w1.shape[2]
    tf = _tile(f, 512)
    specs, nf = _ffn_specs(tm, d, f, tf)
    row = lambda i, c, te, nv: (i, 0)
    return pl.pallas_call(
        _ffn_moe_kernel,
        grid_spec=pltpu.PrefetchScalarGridSpec(
            num_scalar_prefetch=2, grid=(p // tm, nf),
            in_specs=specs + [pl.BlockSpec((tm, 1), row)],
            out_specs=pl.BlockSpec((tm, d), row),
            scratch_shapes=[pltpu.VMEM((tm, d), F32)]),
        out_shape=jax.ShapeDtypeStruct((p, d), BF16),
        compiler_params=_params("parallel", "arbitrary"),
        name="ffn_moe",
    )(te, nv, xs, w1, w3, w2, gate_p)


def _mm_res_ln_kernel(a_ref, w_ref, r_ref, g_ref, b_ref, xo_ref, xb_ref, *, alpha):
    h = jnp.dot(a_ref[...], w_ref[...], preferred_element_type=F32)
    _store_res_ln(alpha * r_ref[...] + h, g_ref, b_ref, xo_ref, xb_ref)


def _mm_res_ln(a, w, x, ln_g, ln_b, alpha):
    t, d = x.shape
    k = a.shape[1]
    tm = _tile(t, 512)
    row = lambda i: (i, 0)
    fixed = lambda i: (0, 0)
    return pl.pallas_call(
        functools.partial(_mm_res_ln_kernel, alpha=alpha),
        grid=(t // tm,),
        in_specs=[pl.BlockSpec((tm, k), row), pl.BlockSpec((k, d), fixed), pl.BlockSpec((tm, d), row),
                  pl.BlockSpec((1, d), fixed), pl.BlockSpec((1, d), fixed)],
        out_specs=[pl.BlockSpec((tm, d), row), pl.BlockSpec((tm, d), row)],
        out_shape=[jax.ShapeDtypeStruct((t, d), F32), jax.ShapeDtypeStruct((t, d), BF16)],
        compiler_params=_params("parallel"),
        name="mm_res_ln",
    )(a, w, x, ln_g, ln_b)


def _add_res_ln_kernel(r_ref, o0_ref, o1_ref, g_ref, b_ref, xo_ref, xb_ref, *, alpha):
    y = alpha * r_ref[...] + (o0_ref[...].astype(F32) + o1_ref[...].astype(F32))
    _store_res_ln(y, g_ref, b_ref, xo_ref, xb_ref)


def _add_res_ln(x, o0, o1, ln_g, ln_b, alpha):
    t, d = x.shape
    tm = _tile(t, 1024)
    row = lambda i: (i, 0)
    fixed = lambda i: (0, 0)
    return pl.pallas_call(
        functools.partial(_add_res_ln_kernel, alpha=alpha),
        grid=(t // tm,),
        in_specs=[pl.BlockSpec((tm, d), row), pl.BlockSpec((tm, d), row), pl.BlockSpec((tm, d), row),
                  pl.BlockSpec((1, d), fixed), pl.BlockSpec((1, d), fixed)],
        out_specs=[pl.BlockSpec((tm, d), row), pl.BlockSpec((tm, d), row)],
        out_shape=[jax.ShapeDtypeStruct((t, d), F32), jax.ShapeDtypeStruct((t, d), BF16)],
        compiler_params=_params("parallel"),
        name="add_res_ln",
    )(x, o0, o1, ln_g, ln_b)


def _qkv_kernel(x_ref, w_ref, qg_ref, kg_ref, cos_ref, sa_ref, sb_ref, q_ref, kt_ref, v_ref, *, n_q):
    tm = x_ref.shape[0]
    acc = jnp.dot(x_ref[...], w_ref[...], preferred_element_type=F32)
    lane = lax.broadcasted_iota(jnp.int32, (tm, LANES), 1)
    left = lane < HEAD_DIM
    cos, sin_a, sin_b = cos_ref[...], sa_ref[...], sb_ref[...]

    def norm_rope(c, g, scale):
        s = c * c
        ss_l = jnp.sum(jnp.where(left, s, 0.0), axis=-1, keepdims=True)
        ss_r = jnp.sum(jnp.where(left, 0.0, s), axis=-1, keepdims=True)
        r = lax.rsqrt(jnp.where(left, ss_l, ss_r) * (1.0 / HEAD_DIM) + RMS_EPS)
        cn = c * r * g
        y = cn * cos + pltpu.roll(cn, LANES - HEAD_DIM // 4, 1) * sin_a + pltpu.roll(cn, HEAD_DIM // 4, 1) * sin_b
        return y * scale

    def split_heads(y, fill):
        return jnp.where(left, y, fill), jnp.where(left, pltpu.roll(y, HEAD_DIM, 1), fill)

    zero = jnp.zeros((tm, LANES), F32)
    for j in range(n_q // 2):
        y = norm_rope(acc[:, j * LANES:(j + 1) * LANES], qg_ref[...], HEAD_DIM ** -0.5)
        even, odd = split_heads(y, zero)
        q_ref[:, (2 * j) * LANES:(2 * j + 1) * LANES] = even.astype(BF16)
        q_ref[:, (2 * j + 1) * LANES:(2 * j + 2) * LANES] = odd.astype(BF16)
    k_off = n_q * HEAD_DIM
    v_off = k_off + N_KV_HEADS * HEAD_DIM
    ones_col = jnp.where(lane == HEAD_DIM, 1.0, 0.0)
    for j in range(N_KV_HEADS // 2):
        y = norm_rope(acc[:, k_off + j * LANES:k_off + (j + 1) * LANES], kg_ref[...], 1.0)
        even, odd = split_heads(y, zero)
        kt_ref[(2 * j) * LANES:(2 * j + 1) * LANES, :] = even.T.astype(BF16)
        kt_ref[(2 * j + 1) * LANES:(2 * j + 2) * LANES, :] = odd.T.astype(BF16)
        even, odd = split_heads(acc[:, v_off + j * LANES:v_off + (j + 1) * LANES], ones_col)
        v_ref[:, (2 * j) * LANES:(2 * j + 1) * LANES] = even.astype(BF16)
        v_ref[:, (2 * j + 1) * LANES:(2 * j + 2) * LANES] = odd.astype(BF16)


def _rope_tables(n):
    pos = jnp.arange(n)
    row = (pos // GRID_W).astype(F32)
    col = (pos % GRID_W).astype(F32)
    sec = HEAD_DIM // 2
    inv = ROPE_THETA ** (-jnp.arange(0, sec, 2, dtype=F32) / sec)
    ang_r = row[:, None] * inv[None, :]
    ang_c = col[:, None] * inv[None, :]
    ang = jnp.concatenate([ang_r, ang_r, ang_c, ang_c], axis=-1)
    ang = jnp.concatenate([ang, ang], axis=-1)
    first = (jnp.arange(LANES) % (HEAD_DIM // 2)) < HEAD_DIM // 4
    sin = jnp.sin(ang)
    return jnp.cos(ang), jnp.where(first, -sin, 0.0), jnp.where(first, 0.0, sin)


def _qkv(xb, w_qkv, q_g2, k_g2, t1, n1, n2):
    t, d = xb.shape
    n_q = d // HEAD_DIM
    tm = _tile(math.gcd(n1, n2), 512)
    cos, sin_a, sin_b = _rope_tables(max(n1, n2))
    nb1 = t1 // tm

    def tab(i):
        return (jnp.where(i < nb1, lax.rem(i, n1 // tm), lax.rem(i - nb1, n2 // tm)), 0)

    row = lambda i: (i, 0)
    fixed = lambda i: (0, 0)
    return pl.pallas_call(
        functools.partial(_qkv_kernel, n_q=n_q),
        grid=(t // tm,),
        in_specs=[pl.BlockSpec((tm, d), row), pl.BlockSpec(w_qkv.shape, fixed),
                  pl.BlockSpec((1, LANES), fixed), pl.BlockSpec((1, LANES), fixed),
                  pl.BlockSpec((tm, LANES), tab), pl.BlockSpec((tm, LANES), tab), pl.BlockSpec((tm, LANES), tab)],
        out_specs=[pl.BlockSpec((tm, n_q * LANES), row),
                   pl.BlockSpec((N_KV_HEADS * LANES, tm), lambda i: (0, i)),
                   pl.BlockSpec((tm, N_KV_HEADS * LANES), row)],
        out_shape=[jax.ShapeDtypeStruct((t, n_q * LANES), BF16),
                   jax.ShapeDtypeStruct((N_KV_HEADS * LANES, t), BF16),
                   jax.ShapeDtypeStruct((t, N_KV_HEADS * LANES), BF16)],
        compiler_params=_params("parallel"),
        name="qkv_rope",
    )(xb, w_qkv, q_g2, k_g2, cos, sin_a, sin_b)


def _attn_kernel(q_ref, kt_ref, v_ref, o_ref, *, q_per_kv):
    tq = q_ref.shape[0]
    kt = kt_ref[...]
    va = v_ref[...]
    left = lax.broadcasted_iota(jnp.int32, (tq, LANES), 1) < HEAD_DIM
    outs = []
    for h in range(q_per_kv):
        s = jnp.dot(q_ref[:, h * LANES:(h + 1) * LANES], kt, preferred_element_type=F32)
        m = jnp.max(s, axis=-1, keepdims=True)
        p = jnp.exp(s - m).astype(BF16)
        o = jnp.dot(p, va, preferred_element_type=F32)
        outs.append(o * (1.0 / o[:, HEAD_DIM:HEAD_DIM + 1]))
    for j in range(q_per_kv // 2):
        pair = jnp.where(left, outs[2 * j], pltpu.roll(outs[2 * j + 1], HEAD_DIM, 1))
        o_ref[:, j * LANES:(j + 1) * LANES] = pair.astype(BF16)


def _attention_group(q, kt, va, o_prev, blk0, bsz, n, d):
    t = q.shape[0]
    q_per_kv = d // HEAD_DIM // N_KV_HEADS
    tq = _tile(n, 256)
    nq = n // tq
    qw = q_per_kv * LANES
    ow = q_per_kv * HEAD_DIM
    in_specs = [pl.BlockSpec((tq, qw), lambda b, h, i: ((blk0 + b) * nq + i, h)),
                pl.BlockSpec((LANES, n), lambda b, h, i: (h, blk0 + b)),
                pl.BlockSpec((n, LANES), lambda b, h, i: (blk0 + b, h))]
    args = [q, kt, va]
    aliases = {}
    if o_prev is not None:
        in_specs.append(pl.BlockSpec(memory_space=pl.ANY))
        args.append(o_prev)
        aliases = {3: 0}
    kern = functools.partial(_attn_kernel, q_per_kv=q_per_kv)
    if o_prev is not None:
        kern = functools.partial(_attn_alias_kernel, q_per_kv=q_per_kv)
    return pl.pallas_call(
        kern,
        grid=(bsz, N_KV_HEADS, nq),
        in_specs=in_specs,
        out_specs=pl.BlockSpec((tq, ow), lambda b, h, i: ((blk0 + b) * nq + i, h)),
        out_shape=jax.ShapeDtypeStruct((t, d), BF16),
        input_output_aliases=aliases,
        compiler_params=_params("parallel", "parallel", "arbitrary"),
        name="attention",
    )(*args)


def _attn_alias_kernel(q_ref, kt_ref, v_ref, prev_ref, o_ref, *, q_per_kv):
    del prev_ref
    _attn_kernel(q_ref, kt_ref, v_ref, o_ref, q_per_kv=q_per_kv)


def _pool_kernel(x_ref, prev_ref, next_ref, w_ref, cb_ref, cs_ref, g_ref, b_ref, xo_ref, xb_ref,
                 ext_ref, h_ref, *, alpha, t1, n1, n2):
    tm, d = x_ref.shape
    halo = prev_ref.shape[0]
    gdim = d // len(POOL_WINDOWS)
    n, pos0 = _seq_pos(pl.program_id(0) * tm, t1, n1, n2)
    x = x_ref[...]
    ext_ref[0:halo, :] = jnp.where(pos0 > 0, prev_ref[...], 0.0)
    ext_ref[halo:halo + tm, :] = x
    ext_ref[halo + tm:halo + tm + halo, :] = jnp.where(pos0 + tm < n, next_ref[...], 0.0)
    pos = pos0 + lax.broadcasted_iota(jnp.int32, (tm, 1), 0)
    for gi, w in enumerate(POOL_WINDOWS):
        cols = slice(gi * gdim, (gi + 1) * gdim)
        wsum = ext_ref[halo - w // 2:halo - w // 2 + tm, cols]
        for j in range(1 - w // 2, w - w // 2):
            wsum = wsum + ext_ref[halo + j:halo + j + tm, cols]
        cnt = (jnp.minimum(pos + (w - w // 2), n) - jnp.maximum(pos - w // 2, 0)).astype(F32)
        mix = (wsum * (1.0 / cnt) - x[:, cols]).astype(BF16)
        y = jnp.dot(mix, w_ref[gi], preferred_element_type=F32) + cb_ref[:, cols]
        h_ref[:, cols] = y * cs_ref[:, cols]
    _store_res_ln(alpha * x + h_ref[...], g_ref, b_ref, xo_ref, xb_ref)


def _pool_mixer(x, c_w, c_b, c_scale, ln_g, ln_b, alpha, t1, n1, n2):
    t, d = x.shape
    halo = SUBLANES_F32
    assert max(POOL_WINDOWS) // 2 <= halo
    tm = _tile(math.gcd(n1, n2), 512)
    hb = tm // halo
    last = t // halo - 1
    row = lambda i: (i, 0)
    fixed = lambda i: (0, 0)
    gdim = d // len(POOL_WINDOWS)
    return pl.pallas_call(
        functools.partial(_pool_kernel, alpha=alpha, t1=t1, n1=n1, n2=n2),
        grid=(t // tm,),
        in_specs=[pl.BlockSpec((tm, d), row),
                  pl.BlockSpec((halo, d), lambda i: (jnp.maximum(i * hb - 1, 0), 0)),
                  pl.BlockSpec((halo, d), lambda i: (jnp.minimum((i + 1) * hb, last), 0)),
                  pl.BlockSpec((len(POOL_WINDOWS), gdim, gdim), lambda i: (0, 0, 0)),
                  pl.BlockSpec((1, d), fixed), pl.BlockSpec((1, d), fixed),
                  pl.BlockSpec((1, d), fixed), pl.BlockSpec((1, d), fixed)],
        out_specs=[pl.BlockSpec((tm, d), row), pl.BlockSpec((tm, d), row)],
        out_shape=[jax.ShapeDtypeStruct((t, d), F32), jax.ShapeDtypeStruct((t, d), BF16)],
        scratch_shapes=[pltpu.VMEM((tm + 2 * halo, d), F32), pltpu.VMEM((tm, d), F32)],
        compiler_params=_params("parallel"),
        name="pool_mixer",
    )(x, x, x, c_w, c_b, c_scale, ln_g, ln_b)


def _conv_kernel(h_ref, prev_ref, next_ref, x_ref, wdw_ref, bdw_ref, cg_ref, cb_ref, wo_ref, bo_ref, g_ref, b_ref,
                 xo_ref, xb_ref, ext_ref, conv_ref, *, alpha, t1, n1, n2, rows, cols):
    tm, d = h_ref.shape
    halo = prev_ref.shape[0]
    width = wdw_ref.shape[0]
    n, pos0 = _seq_pos(pl.program_id(0) * tm, t1, n1, n2)
    ext_ref[0:halo, :] = jnp.where(pos0 > 0, prev_ref[...].astype(F32), 0.0)
    ext_ref[halo:halo + tm, :] = h_ref[...].astype(F32)
    ext_ref[halo + tm:halo + tm + halo, :] = jnp.where(pos0 + tm < n, next_ref[...].astype(F32), 0.0)
    first = halo - width // 2

    def chunk(r, carry):
        r0 = pl.multiple_of(r * rows, rows)
        for c0 in range(0, d, cols):
            win = ext_ref[pl.ds(r0, rows + 2 * halo), c0:c0 + cols]
            acc = win[first:first + rows] * wdw_ref[0:1, c0:c0 + cols]
            for k in range(1, width):
                acc = acc + win[first + k:first + k + rows] * wdw_ref[k:k + 1, c0:c0 + cols]
            conv_ref[pl.ds(r0, rows), c0:c0 + cols] = acc
        return carry

    lax.fori_loop(0, tm // rows, chunk, 0)
    c = _silu(_ln_rows(conv_ref[...] + bdw_ref[...], cg_ref[...], cb_ref[...])).astype(BF16)
    h = jnp.dot(c, wo_ref[...], preferred_element_type=F32) + bo_ref[...]
    _store_res_ln(alpha * x_ref[...] + h, g_ref, b_ref, xo_ref, xb_ref)


def _conv_mixer(hg, x, w_dw, b_dw, c_g, c_b, w_out, b_out, ln_g, ln_b, alpha, t1, n1, n2):
    t, d = x.shape
    halo = SUBLANES_BF16
    assert w_dw.shape[0] // 2 <= halo
    tm = _tile(math.gcd(n1, n2), 512)
    hb = tm // halo
    last = t // halo - 1
    row = lambda i: (i, 0)
    fixed = lambda i: (0, 0)
    vec = pl.BlockSpec((1, d), fixed)
    return pl.pallas_call(
        functools.partial(_conv_kernel, alpha=alpha, t1=t1, n1=n1, n2=n2,
                          rows=SUBLANES_BF16, cols=4 * LANES),
        grid=(t // tm,),
        in_specs=[pl.BlockSpec((tm, d), row),
                  pl.BlockSpec((halo, d), lambda i: (jnp.maximum(i * hb - 1, 0), 0)),
                  pl.BlockSpec((halo, d), lambda i: (jnp.minimum((i + 1) * hb, last), 0)),
                  pl.BlockSpec((tm, d), row),
                  pl.BlockSpec(w_dw.shape, fixed), vec, vec, vec,
                  pl.BlockSpec((d, d), fixed), vec, vec, vec],
        out_specs=[pl.BlockSpec((tm, d), row), pl.BlockSpec((tm, d), row)],
        out_shape=[jax.ShapeDtypeStruct((t, d), F32), jax.ShapeDtypeStruct((t, d), BF16)],
        scratch_shapes=[pltpu.VMEM((tm + 2 * halo, d), F32), pltpu.VMEM((tm, d), F32)],
        compiler_params=_params("parallel"),
        name="conv_mixer",
    )(hg, hg, hg, x, w_dw, b_dw, c_g, c_b, w_out, b_out, ln_g, ln_b)


def _router_kernel(x_ref, w_ref, b_ref, idx_ref, gate_ref, *, n_experts):
    x = x_ref[...]
    w = w_ref[...]
    x_hi = x.astype(BF16)
    x_lo = (x - x_hi.astype(F32)).astype(BF16)
    w_hi = w.astype(BF16)
    w_lo = (w - w_hi.astype(F32)).astype(BF16)
    logits = (jnp.dot(x_hi, w_hi, preferred_element_type=F32)
              + (jnp.dot(x_hi, w_lo, preferred_element_type=F32) + jnp.dot(x_lo, w_hi, preferred_element_type=F32))
              + b_ref[...])
    lane = lax.broadcasted_iota(jnp.int32, logits.shape, 1)
    neg = jnp.float32(-jnp.inf)
    logits = jnp.where(lane < n_experts, logits, neg)
    m1 = jnp.max(logits, axis=-1, keepdims=True)
    i1 = jnp.min(jnp.where(logits == m1, lane, LANES), axis=-1, keepdims=True)
    rest = jnp.where(lane == i1, neg, logits)
    m2 = jnp.max(rest, axis=-1, keepdims=True)
    i2 = jnp.min(jnp.where(rest == m2, lane, LANES), axis=-1, keepdims=True)
    e2 = jnp.exp(m2 - m1)
    den = 1.0 + e2
    idx_ref[...] = jnp.where(lane == 0, i1, jnp.where(lane == 1, i2, 0))
    gate_ref[...] = jnp.where(lane == 0, 1.0 / den, jnp.where(lane == 1, e2 / den, 0.0))


def _router(x, w_r, b_r):
    t, d = x.shape
    n_experts = w_r.shape[1]
    w_pad = jnp.zeros((d, LANES), F32).at[:, :n_experts].set(w_r)
    b_pad = jnp.zeros((1, LANES), F32).at[0, :n_experts].set(b_r)
    tm = _tile(t, 1024)
    row = lambda i: (i, 0)
    fixed = lambda i: (0, 0)
    idx, gate = pl.pallas_call(
        functools.partial(_router_kernel, n_experts=n_experts),
        grid=(t // tm,),
        in_specs=[pl.BlockSpec((tm, d), row), pl.BlockSpec((d, LANES), fixed), pl.BlockSpec((1, LANES), fixed)],
        out_specs=[pl.BlockSpec((tm, LANES), row), pl.BlockSpec((tm, LANES), row)],
        out_shape=[jax.ShapeDtypeStruct((t, LANES), jnp.int32), jax.ShapeDtypeStruct((t, LANES), F32)],
        compiler_params=_params("parallel"),
        name="router",
    )(x, w_pad, b_pad)
    return idx[:, :TOP_K], gate[:, :TOP_K]


def _moe_dispatch(idx, gate, n_experts, tm):
    t = idx.shape[0]
    s = t * TOP_K
    e_flat = idx.reshape(s)
    onehot = (e_flat[:, None] == jnp.arange(n_experts, dtype=jnp.int32)[None, :]).astype(jnp.int32)
    csum = jnp.cumsum(onehot, axis=0)
    rank = jnp.sum(jnp.where(onehot > 0, csum, 0), axis=1) - 1
    counts = csum[-1]
    padded = ((counts + tm - 1) // tm) * tm
    pend = jnp.cumsum(padded)
    dest = (pend - padded)[e_flat] + rank
    n_tiles = s // tm + n_experts
    p = n_tiles * tm
    src_tok = jnp.zeros((p,), jnp.int32).at[dest].set(jnp.arange(s, dtype=jnp.int32) // TOP_K)
    gate_p = jnp.zeros((p,), F32).at[dest].set(gate.reshape(s))
    nv = (pend[-1] // tm).astype(jnp.int32)
    tile_idx = jnp.arange(n_tiles, dtype=jnp.int32)
    te = jnp.searchsorted(pend, jnp.minimum(tile_idx, nv - 1) * tm, side="right").astype(jnp.int32)
    return src_tok, gate_p.reshape(p, 1), te, nv.reshape(1), dest.reshape(t, TOP_K)


def _moe_layer(x, xb, w_r, b_r, w1, w3, w2, ln_g, ln_b, alpha):
    t, d = x.shape
    n_experts = w_r.shape[1]
    tm = _tile(t * TOP_K, 1024)
    idx, gate = _router(x, w_r, b_r)
    src_tok, gate_p, te, nv, slot = _moe_dispatch(idx, gate, n_experts, tm)
    xs = jnp.take(xb, src_tok, axis=0)
    out_p = _ffn_moe(xs, gate_p, te, nv, w1, w3, w2, tm)
    o0 = jnp.take(out_p, slot[:, 0], axis=0)
    o1 = jnp.take(out_p, slot[:, 1], axis=0)
    return _add_res_ln(x, o0, o1, ln_g, ln_b, alpha)


def kernel(x_prompt, x_sample, ln_mix_g, ln_mix_b, ln_ffn_g, ln_ffn_b, a_w_in, a_b_in, a_ln_g, a_ln_b, a_w_s, a_b_s, a_w_out, b_w_qkv, b_q_g, b_k_g, b_w_o, c_w, c_b, c_scale, d_w_in, d_b_in, d_w_dw, d_b_dw, d_ln_g, d_ln_b, d_w_out, d_b_out, f_w1, f_w3, f_w2, m_w_r, m_b_r, m_w1, m_w3, m_w2):
    bsz1, n1, d = x_prompt.shape
    bsz2, n2, _ = x_sample.shape
    t1, t2 = bsz1 * n1, bsz2 * n2
    depth = ln_mix_g.shape[0]
    alpha = (2.0 * depth) ** 0.25
    vec = lambda v: v.reshape(1, -1).astype(F32)
    bf = lambda w: w.astype(BF16)

    x = jnp.concatenate([x_prompt.reshape(t1, d), x_sample.reshape(t2, d)], axis=0)
    xb = x.astype(BF16)
    for l in range(depth):
        kind, i = l % 4, l // 4
        mg, mb = vec(ln_mix_g[l]), vec(ln_mix_b[l])
        if kind == 0:
            z = _mm_gelu(xb, bf(a_w_in[i]), vec(a_b_in[i]))
            x, xb = _gmlp_gate(z, x, vec(a_ln_g[i]), vec(a_ln_b[i]), bf(a_w_s[i]), a_b_s[i].T.astype(F32),
                               bf(a_w_out[i]), mg, mb, alpha)
        elif kind == 1:
            two_heads = lambda g: jnp.tile(g.astype(F32), 2).reshape(1, LANES)
            q, kt, va = _qkv(xb, bf(b_w_qkv[i]), two_heads(b_q_g[i]), two_heads(b_k_g[i]), t1, n1, n2)
            o = _attention_group(q, kt, va, None, 0, bsz1, n1, d)
            o = _attention_group(q, kt, va, o, t1 // n2, bsz2, n2, d)
            x, xb = _mm_res_ln(o, bf(b_w_o[i]), x, mg, mb, alpha)
        elif kind == 2:
            x, xb = _pool_mixer(x, bf(c_w[i]), vec(c_b[i]), vec(c_scale[i]), mg, mb, alpha, t1, n1, n2)
        else:
            hg = _mm_glu(xb, bf(d_w_in[i]), vec(d_b_in[i]))
            x, xb = _conv_mixer(hg, x, d_w_dw[i].astype(F32), vec(d_b_dw[i]), vec(d_ln_g[i]), vec(d_ln_b[i]),
                                bf(d_w_out[i]), vec(d_b_out[i]), mg, mb, alpha, t1, n1, n2)
        fg, fb = vec(ln_ffn_g[l]), vec(ln_ffn_b[l])
        j = l // 2
        if l % 2 == 0:
            x, xb = _ffn_dense(xb, x, bf(f_w1[j:j + 1]), bf(f_w3[j:j + 1]), bf(f_w2[j:j + 1]), fg, fb, alpha)
        else:
            x, xb = _moe_layer(x, xb, m_w_r[j], m_b_r[j], bf(m_w1[j]), bf(m_w3[j]), bf(m_w2[j]), fg, fb, alpha)
    return x[:t1].reshape(bsz1, n1, d), x[t1:].reshape(bsz2, n2, d)
```

```python
import functools
import math

import jax
import jax.numpy as jnp
from jax import lax
from jax.experimental import pallas as pl
from jax.experimental.pallas import tpu as pltpu

F32 = jnp.float32
BF16 = jnp.bfloat16

GRID_W = 64
BLOCK = 128
A_GROUPS = 8
HEAD_DIM = 64
N_KV_HEADS = 4
ROPE_THETA = 10000.0
POOL_WINDOWS = (2, 4, 8, 16)
TOP_K = 2
LN_EPS = 1e-5
RMS_EPS = 1e-6

LANES = 128
SUBLANES_F32 = 8
SUBLANES_BF16 = 16
VMEM_LIMIT = 56 * 1024 * 1024


def _params(*sem):
    return pltpu.CompilerParams(dimension_semantics=sem, vmem_limit_bytes=VMEM_LIMIT)


def _tile(total, target):
    t = math.gcd(total, target)
    assert t % SUBLANES_BF16 == 0, (total, target)
    return t


def _ln_rows(y, g, b):
    mu = jnp.mean(y, axis=-1, keepdims=True)
    yc = y - mu
    var = jnp.mean(yc * yc, axis=-1, keepdims=True)
    return yc * lax.rsqrt(var + LN_EPS) * g + b


def _gelu_tanh(x):
    c = math.sqrt(2.0 / math.pi)
    return x * (0.5 * (1.0 + jnp.tanh(c * (x + 0.044715 * (x * x * x)))))


def _silu(x):
    return x * jax.nn.sigmoid(x)


def _seq_pos(t0, t1, n1, n2):
    in_g1 = t0 < t1
    n = jnp.where(in_g1, n1, n2)
    base = jnp.where(in_g1, 0, t1)
    return n, lax.rem(t0 - base, n)


def _store_res_ln(y, g_ref, b_ref, xo_ref, xb_ref):
    out = _ln_rows(y, g_ref[...], b_ref[...])
    xo_ref[...] = out
    xb_ref[...] = out.astype(BF16)


def _mm_gelu_kernel(x_ref, w_ref, b_ref, o_ref):
    z = jnp.dot(x_ref[...], w_ref[...], preferred_element_type=F32) + b_ref[...]
    o_ref[...] = _gelu_tanh(z).astype(o_ref.dtype)


def _mm_gelu(xb, w, b):
    t, k = xb.shape
    n = w.shape[1]
    tm, tn = _tile(t, 1024), _tile(n, 1024)
    return pl.pallas_call(
        _mm_gelu_kernel,
        grid=(n // tn, t // tm),
        in_specs=[pl.BlockSpec((tm, k), lambda j, i: (i, 0)),
                  pl.BlockSpec((k, tn), lambda j, i: (0, j)),
                  pl.BlockSpec((1, tn), lambda j, i: (0, j))],
        out_specs=pl.BlockSpec((tm, tn), lambda j, i: (i, j)),
        out_shape=jax.ShapeDtypeStruct((t, n), BF16),
        compiler_params=_params("parallel", "parallel"),
        name="mm_gelu",
    )(xb, w, b)


def _mm_glu_kernel(x_ref, wa_ref, wg_ref, ba_ref, bg_ref, o_ref):
    x = x_ref[...]
    a = jnp.dot(x, wa_ref[...], preferred_element_type=F32) + ba_ref[...]
    g = jnp.dot(x, wg_ref[...], preferred_element_type=F32) + bg_ref[...]
    o_ref[...] = (a * jax.nn.sigmoid(g)).astype(o_ref.dtype)


def _mm_glu(xb, w, b):
    t, k = xb.shape
    n = w.shape[1] // 2
    tm, tn = _tile(t, 1024), _tile(n, 512)
    nj = n // tn
    return pl.pallas_call(
        _mm_glu_kernel,
        grid=(nj, t // tm),
        in_specs=[pl.BlockSpec((tm, k), lambda j, i: (i, 0)),
                  pl.BlockSpec((k, tn), lambda j, i: (0, j)),
                  pl.BlockSpec((k, tn), lambda j, i: (0, j + nj)),
                  pl.BlockSpec((1, tn), lambda j, i: (0, j)),
                  pl.BlockSpec((1, tn), lambda j, i: (0, j + nj))],
        out_specs=pl.BlockSpec((tm, tn), lambda j, i: (i, j)),
        out_shape=jax.ShapeDtypeStruct((t, n), BF16),
        compiler_params=_params("parallel", "parallel"),
        name="mm_glu",
    )(xb, w, w, b, b)


def _gmlp_gate_kernel(z_ref, x_ref, vg_ref, vb_ref, ws_ref, bs_ref, wo_ref, g_ref, b_ref,
                      xo_ref, xb_ref, gated_ref, *, alpha, half, gdim):
    tm = z_ref.shape[0]
    v = _ln_rows(z_ref[:, half:].astype(F32), vg_ref[...], vb_ref[...]).astype(BF16)
    for c in range(tm // BLOCK):
        rows = slice(c * BLOCK, (c + 1) * BLOCK)
        for gi in range(A_GROUPS):
            cols = slice(gi * gdim, (gi + 1) * gdim)
            sv = jnp.dot(ws_ref[gi], v[rows, cols], preferred_element_type=F32) + bs_ref[:, gi:gi + 1]
            gated_ref[rows, cols] = (z_ref[rows, cols].astype(F32) * sv).astype(BF16)
    h = jnp.dot(gated_ref[...], wo_ref[...], preferred_element_type=F32)
    _store_res_ln(alpha * x_ref[...] + h, g_ref, b_ref, xo_ref, xb_ref)


def _gmlp_gate(z, x, v_g, v_b, w_s, b_s_t, w_out, ln_g, ln_b, alpha):
    t, d = x.shape
    half = z.shape[1] // 2
    tm = _tile(t, 2 * BLOCK)
    row = lambda i: (i, 0)
    fixed2 = lambda i: (0, 0)
    return pl.pallas_call(
        functools.partial(_gmlp_gate_kernel, alpha=alpha, half=half, gdim=half // A_GROUPS),
        grid=(t // tm,),
        in_specs=[pl.BlockSpec((tm, 2 * half), row),
                  pl.BlockSpec((tm, d), row),
                  pl.BlockSpec((1, half), fixed2),
                  pl.BlockSpec((1, half), fixed2),
                  pl.BlockSpec((A_GROUPS, BLOCK, BLOCK), lambda i: (0, 0, 0)),
                  pl.BlockSpec((BLOCK, A_GROUPS), fixed2),
                  pl.BlockSpec((half, d), fixed2),
                  pl.BlockSpec((1, d), fixed2),
                  pl.BlockSpec((1, d), fixed2)],
        out_specs=[pl.BlockSpec((tm, d), row), pl.BlockSpec((tm, d), row)],
        out_shape=[jax.ShapeDtypeStruct((t, d), F32), jax.ShapeDtypeStruct((t, d), BF16)],
        scratch_shapes=[pltpu.VMEM((tm, half), BF16)],
        compiler_params=_params("parallel"),
        name="gmlp_gate",
    )(z, x, v_g, v_b, w_s, b_s_t, w_out, ln_g, ln_b)


def _ffn_accumulate(i, nv_ref, x_ref, w1_ref, w3_ref, w2_ref, acc_ref):
    c = pl.program_id(1)

    @pl.when(c == 0)
    def _():
        acc_ref[...] = jnp.zeros_like(acc_ref)

    @pl.when(i < nv_ref[0])
    def _():
        x = x_ref[...]
        a = jnp.dot(x, w1_ref[0], preferred_element_type=F32)
        b = jnp.dot(x, w3_ref[0], preferred_element_type=F32)
        mid = (_silu(a) * b).astype(BF16)
        acc_ref[...] += jnp.dot(mid, w2_ref[0], preferred_element_type=F32)


def _ffn_dense_kernel(te_ref, nv_ref, x_ref, w1_ref, w3_ref, w2_ref, r_ref, g_ref, b_ref,
                      xo_ref, xb_ref, acc_ref, *, alpha):
    _ffn_accumulate(pl.program_id(0), nv_ref, x_ref, w1_ref, w3_ref, w2_ref, acc_ref)

    @pl.when(pl.program_id(1) == pl.num_programs(1) - 1)
    def _():
        _store_res_ln(alpha * r_ref[...] + acc_ref[...], g_ref, b_ref, xo_ref, xb_ref)


def _ffn_moe_kernel(te_ref, nv_ref, x_ref, w1_ref, w3_ref, w2_ref, o_ref, acc_ref):
    _ffn_accumulate(pl.program_id(0), nv_ref, x_ref, w1_ref, w3_ref, w2_ref, acc_ref)

    @pl.when(pl.program_id(1) == pl.num_programs(1) - 1)
    def _():
        o_ref[...] = acc_ref[...].astype(o_ref.dtype)


def _ffn_specs(tm, d, f, tf):
    nf = f // tf

    def chunk(i, c, te, nv):
        return jnp.where(i < nv[0], c, nf - 1)

    def row(i, c, te, nv):
        return (jnp.minimum(i, nv[0] - 1), 0)

    return [pl.BlockSpec((tm, d), row),
            pl.BlockSpec((1, d, tf), lambda i, c, te, nv: (te[i], 0, chunk(i, c, te, nv))),
            pl.BlockSpec((1, d, tf), lambda i, c, te, nv: (te[i], 0, chunk(i, c, te, nv))),
            pl.BlockSpec((1, tf, d), lambda i, c, te, nv: (te[i], chunk(i, c, te, nv), 0))], nf


def _ffn_dense(xb, x, w1, w3, w2, ln_g, ln_b, alpha):
    t, d = x.shape
    f = w1.shape[2]
    tm, tf = _tile(t, 1024), _tile(f, 512)
    specs, nf = _ffn_specs(tm, d, f, tf)
    row = lambda i, c, te, nv: (i, 0)
    fixed = lambda i, c, te, nv: (0, 0)
    te = jnp.zeros((t // tm,), jnp.int32)
    nv = jnp.full((1,), t // tm, jnp.int32)
    return pl.pallas_call(
        functools.partial(_ffn_dense_kernel, alpha=alpha),
        grid_spec=pltpu.PrefetchScalarGridSpec(
            num_scalar_prefetch=2, grid=(t // tm, nf),
            in_specs=specs + [pl.BlockSpec((tm, d), row), pl.BlockSpec((1, d), fixed), pl.BlockSpec((1, d), fixed)],
            out_specs=[pl.BlockSpec((tm, d), row), pl.BlockSpec((tm, d), row)],
            scratch_shapes=[pltpu.VMEM((tm, d), F32)]),
        out_shape=[jax.ShapeDtypeStruct((t, d), F32), jax.ShapeDtypeStruct((t, d), BF16)],
        compiler_params=_params("parallel", "arbitrary"),
        name="ffn_dense",
    )(te, nv, xb, w1, w3, w2, x, ln_g, ln_b)


def _ffn_moe(xs, te, nv, w1, w3, w2, tm):
    p, d = xs.shape
    f = w1.shape[2]
    tf = _tile(f, 512)
    specs, nf = _ffn_specs(tm, d, f, tf)
    row = lambda i, c, te, nv: (i, 0)
    return pl.pallas_call(
        _ffn_moe_kernel,
        grid_spec=pltpu.PrefetchScalarGridSpec(
            num_scalar_prefetch=2, grid=(p // tm, nf),
            in_specs=specs,
            out_specs=pl.BlockSpec((tm, d), row),
            scratch_shapes=[pltpu.VMEM((tm, d), F32)]),
        out_shape=jax.ShapeDtypeStruct((p, d), BF16),
        compiler_params=_params("parallel", "arbitrary"),
        name="ffn_moe",
    )(te, nv, xs, w1, w3, w2)


def _mm_res_ln_kernel(a_ref, w_ref, r_ref, g_ref, b_ref, xo_ref, xb_ref, *, alpha):
    h = jnp.dot(a_ref[...], w_ref[...], preferred_element_type=F32)
    _store_res_ln(alpha * r_ref[...] + h, g_ref, b_ref, xo_ref, xb_ref)


def _mm_res_ln(a, w, x, ln_g, ln_b, alpha):
    t, d = x.shape
    k = a.shape[1]
    tm = _tile(t, 512)
    row = lambda i: (i, 0)
    fixed = lambda i: (0, 0)
    return pl.pallas_call(
        functools.partial(_mm_res_ln_kernel, alpha=alpha),
        grid=(t // tm,),
        in_specs=[pl.BlockSpec((tm, k), row), pl.BlockSpec((k, d), fixed), pl.BlockSpec((tm, d), row),
                  pl.BlockSpec((1, d), fixed), pl.BlockSpec((1, d), fixed)],
        out_specs=[pl.BlockSpec((tm, d), row), pl.BlockSpec((tm, d), row)],
        out_shape=[jax.ShapeDtypeStruct((t, d), F32), jax.ShapeDtypeStruct((t, d), BF16)],
        compiler_params=_params("parallel"),
        name="mm_res_ln",
    )(a, w, x, ln_g, ln_b)


def _add_res_ln_kernel(r_ref, o0_ref, o1_ref, gate_ref, g_ref, b_ref, xo_ref, xb_ref, *, alpha):
    h = gate_ref[:, 0:1] * o0_ref[...].astype(F32) + gate_ref[:, 1:2] * o1_ref[...].astype(F32)
    _store_res_ln(alpha * r_ref[...] + h, g_ref, b_ref, xo_ref, xb_ref)


def _add_res_ln(x, o0, o1, gate, ln_g, ln_b, alpha):
    t, d = x.shape
    tm = _tile(t, 1024)
    row = lambda i: (i, 0)
    fixed = lambda i: (0, 0)
    return pl.pallas_call(
        functools.partial(_add_res_ln_kernel, alpha=alpha),
        grid=(t // tm,),
        in_specs=[pl.BlockSpec((tm, d), row), pl.BlockSpec((tm, d), row), pl.BlockSpec((tm, d), row),
                  pl.BlockSpec((tm, LANES), row), pl.BlockSpec((1, d), fixed), pl.BlockSpec((1, d), fixed)],
        out_specs=[pl.BlockSpec((tm, d), row), pl.BlockSpec((tm, d), row)],
        out_shape=[jax.ShapeDtypeStruct((t, d), F32), jax.ShapeDtypeStruct((t, d), BF16)],
        compiler_params=_params("parallel"),
        name="add_res_ln",
    )(x, o0, o1, gate, ln_g, ln_b)


def _qkv_kernel(x_ref, w_ref, qg_ref, kg_ref, cos_ref, sa_ref, sb_ref, q_ref, kt_ref, v_ref, *, n_q):
    tm = x_ref.shape[0]
    acc = jnp.dot(x_ref[...], w_ref[...], preferred_element_type=F32)
    lane = lax.broadcasted_iota(jnp.int32, (tm, LANES), 1)
    left = lane < HEAD_DIM
    cos, sin_a, sin_b = cos_ref[...], sa_ref[...], sb_ref[...]

    def norm_rope(c, g, scale):
        s = c * c
        ss_l = jnp.sum(jnp.where(left, s, 0.0), axis=-1, keepdims=True)
        ss_r = jnp.sum(jnp.where(left, 0.0, s), axis=-1, keepdims=True)
        r = lax.rsqrt(jnp.where(left, ss_l, ss_r) * (1.0 / HEAD_DIM) + RMS_EPS)
        cn = c * r * g
        y = cn * cos + pltpu.roll(cn, LANES - HEAD_DIM // 4, 1) * sin_a + pltpu.roll(cn, HEAD_DIM // 4, 1) * sin_b
        return y * scale

    def split_heads(y, fill):
        return jnp.where(left, y, fill), jnp.where(left, pltpu.roll(y, HEAD_DIM, 1), fill)

    zero = jnp.zeros((tm, LANES), F32)
    for j in range(n_q // 2):
        y = norm_rope(acc[:, j * LANES:(j + 1) * LANES], qg_ref[...], HEAD_DIM ** -0.5)
        even, odd = split_heads(y, zero)
        q_ref[:, (2 * j) * LANES:(2 * j + 1) * LANES] = even.astype(BF16)
        q_ref[:, (2 * j + 1) * LANES:(2 * j + 2) * LANES] = odd.astype(BF16)
    k_off = n_q * HEAD_DIM
    v_off = k_off + N_KV_HEADS * HEAD_DIM
    ones_col = jnp.where(lane == HEAD_DIM, 1.0, 0.0)
    for j in range(N_KV_HEADS // 2):
        y = norm_rope(acc[:, k_off + j * LANES:k_off + (j + 1) * LANES], kg_ref[...], 1.0)
        even, odd = split_heads(y, zero)
        kt_ref[(2 * j) * LANES:(2 * j + 1) * LANES, :] = even.T.astype(BF16)
        kt_ref[(2 * j + 1) * LANES:(2 * j + 2) * LANES, :] = odd.T.astype(BF16)
        even, odd = split_heads(acc[:, v_off + j * LANES:v_off + (j + 1) * LANES], ones_col)
        v_ref[:, (2 * j) * LANES:(2 * j + 1) * LANES] = even.astype(BF16)
        v_ref[:, (2 * j + 1) * LANES:(2 * j + 2) * LANES] = odd.astype(BF16)


def _rope_tables(n):
    pos = jnp.arange(n)
    row = (pos // GRID_W).astype(F32)
    col = (pos % GRID_W).astype(F32)
    sec = HEAD_DIM // 2
    inv = ROPE_THETA ** (-jnp.arange(0, sec, 2, dtype=F32) / sec)
    ang_r = row[:, None] * inv[None, :]
    ang_c = col[:, None] * inv[None, :]
    ang = jnp.concatenate([ang_r, ang_r, ang_c, ang_c], axis=-1)
    ang = jnp.concatenate([ang, ang], axis=-1)
    first = (jnp.arange(LANES) % (HEAD_DIM // 2)) < HEAD_DIM // 4
    sin = jnp.sin(ang)
    return jnp.cos(ang), jnp.where(first, -sin, 0.0), jnp.where(first, 0.0, sin)


def _qkv(xb, w_qkv, q_g2, k_g2, t1, n1, n2):
    t, d = xb.shape
    n_q = d // HEAD_DIM
    tm = _tile(math.gcd(n1, n2), 512)
    cos, sin_a, sin_b = _rope_tables(max(n1, n2))
    nb1 = t1 // tm

    def tab(i):
        return (jnp.where(i < nb1, lax.rem(i, n1 // tm), lax.rem(i - nb1, n2 // tm)), 0)

    row = lambda i: (i, 0)
    fixed = lambda i: (0, 0)
    return pl.pallas_call(
        functools.partial(_qkv_kernel, n_q=n_q),
        grid=(t // tm,),
        in_specs=[pl.BlockSpec((tm, d), row), pl.BlockSpec(w_qkv.shape, fixed),
                  pl.BlockSpec((1, LANES), fixed), pl.BlockSpec((1, LANES), fixed),
                  pl.BlockSpec((tm, LANES), tab), pl.BlockSpec((tm, LANES), tab), pl.BlockSpec((tm, LANES), tab)],
        out_specs=[pl.BlockSpec((tm, n_q * LANES), row),
                   pl.BlockSpec((N_KV_HEADS * LANES, tm), lambda i: (0, i)),
                   pl.BlockSpec((tm, N_KV_HEADS * LANES), row)],
        out_shape=[jax.ShapeDtypeStruct((t, n_q * LANES), BF16),
                   jax.ShapeDtypeStruct((N_KV_HEADS * LANES, t), BF16),
                   jax.ShapeDtypeStruct((t, N_KV_HEADS * LANES), BF16)],
        compiler_params=_params("parallel"),
        name="qkv_rope",
    )(xb, w_qkv, q_g2, k_g2, cos, sin_a, sin_b)


def _attn_kernel(q_ref, kt_ref, v_ref, o_ref, *, q_per_kv):
    tq = q_ref.shape[0]
    kt = kt_ref[...]
    va = v_ref[...]
    left = lax.broadcasted_iota(jnp.int32, (tq, LANES), 1) < HEAD_DIM
    outs = []
    for h in range(q_per_kv):
        s = jnp.dot(q_ref[:, h * LANES:(h + 1) * LANES], kt, preferred_element_type=F32)
        m = jnp.max(s, axis=-1, keepdims=True)
        p = jnp.exp(s - m).astype(BF16)
        o = jnp.dot(p, va, preferred_element_type=F32)
        outs.append(o * (1.0 / o[:, HEAD_DIM:HEAD_DIM + 1]))
    for j in range(q_per_kv // 2):
        pair = jnp.where(left, outs[2 * j], pltpu.roll(outs[2 * j + 1], HEAD_DIM, 1))
        o_ref[:, j * LANES:(j + 1) * LANES] = pair.astype(BF16)


def _attention_group(q, kt, va, o_prev, blk0, bsz, n, d):
    t = q.shape[0]
    q_per_kv = d // HEAD_DIM // N_KV_HEADS
    tq = _tile(n, 256)
    nq = n // tq
    qw = q_per_kv * LANES
    ow = q_per_kv * HEAD_DIM
    in_specs = [pl.BlockSpec((tq, qw), lambda b, h, i: ((blk0 + b) * nq + i, h)),
                pl.BlockSpec((LANES, n), lambda b, h, i: (h, blk0 + b)),
                pl.BlockSpec((n, LANES), lambda b, h, i: (blk0 + b, h))]
    args = [q, kt, va]
    aliases = {}
    if o_prev is not None:
        in_specs.append(pl.BlockSpec(memory_space=pl.ANY))
        args.append(o_prev)
        aliases = {3: 0}
    kern = functools.partial(_attn_kernel, q_per_kv=q_per_kv)
    if o_prev is not None:
        kern = functools.partial(_attn_alias_kernel, q_per_kv=q_per_kv)
    return pl.pallas_call(
        kern,
        grid=(bsz, N_KV_HEADS, nq),
        in_specs=in_specs,
        out_specs=pl.BlockSpec((tq, ow), lambda b, h, i: ((blk0 + b) * nq + i, h)),
        out_shape=jax.ShapeDtypeStruct((t, d), BF16),
        input_output_aliases=aliases,
        compiler_params=_params("parallel", "parallel", "arbitrary"),
        name="attention",
    )(*args)


def _attn_alias_kernel(q_ref, kt_ref, v_ref, prev_ref, o_ref, *, q_per_kv):
    del prev_ref
    _attn_kernel(q_ref, kt_ref, v_ref, o_ref, q_per_kv=q_per_kv)


def _pool_kernel(x_ref, prev_ref, next_ref, w_ref, cb_ref, cs_ref, g_ref, b_ref, xo_ref, xb_ref,
                 ext_ref, h_ref, *, alpha, t1, n1, n2):
    tm, d = x_ref.shape
    halo = prev_ref.shape[0]
    gdim = d // len(POOL_WINDOWS)
    n, pos0 = _seq_pos(pl.program_id(0) * tm, t1, n1, n2)
    x = x_ref[...]
    ext_ref[0:halo, :] = jnp.where(pos0 > 0, prev_ref[...], 0.0)
    ext_ref[halo:halo + tm, :] = x
    ext_ref[halo + tm:halo + tm + halo, :] = jnp.where(pos0 + tm < n, next_ref[...], 0.0)
    pos = pos0 + lax.broadcasted_iota(jnp.int32, (tm, 1), 0)
    for gi, w in enumerate(POOL_WINDOWS):
        cols = slice(gi * gdim, (gi + 1) * gdim)
        wsum = ext_ref[halo - w // 2:halo - w // 2 + tm, cols]
        for j in range(1 - w // 2, w - w // 2):
            wsum = wsum + ext_ref[halo + j:halo + j + tm, cols]
        cnt = (jnp.minimum(pos + (w - w // 2), n) - jnp.maximum(pos - w // 2, 0)).astype(F32)
        mix = (wsum * (1.0 / cnt) - x[:, cols]).astype(BF16)
        y = jnp.dot(mix, w_ref[gi], preferred_element_type=F32) + cb_ref[:, cols]
        h_ref[:, cols] = y * cs_ref[:, cols]
    _store_res_ln(alpha * x + h_ref[...], g_ref, b_ref, xo_ref, xb_ref)


def _pool_mixer(x, c_w, c_b, c_scale, ln_g, ln_b, alpha, t1, n1, n2):
    t, d = x.shape
    halo = SUBLANES_F32
    assert max(POOL_WINDOWS) // 2 <= halo
    tm = _tile(math.gcd(n1, n2), 512)
    hb = tm // halo
    last = t // halo - 1
    row = lambda i: (i, 0)
    fixed = lambda i: (0, 0)
    gdim = d // len(POOL_WINDOWS)
    return pl.pallas_call(
        functools.partial(_pool_kernel, alpha=alpha, t1=t1, n1=n1, n2=n2),
        grid=(t // tm,),
        in_specs=[pl.BlockSpec((tm, d), row),
                  pl.BlockSpec((halo, d), lambda i: (jnp.maximum(i * hb - 1, 0), 0)),
                  pl.BlockSpec((halo, d), lambda i: (jnp.minimum((i + 1) * hb, last), 0)),
                  pl.BlockSpec((len(POOL_WINDOWS), gdim, gdim), lambda i: (0, 0, 0)),
                  pl.BlockSpec((1, d), fixed), pl.BlockSpec((1, d), fixed),
                  pl.BlockSpec((1, d), fixed), pl.BlockSpec((1, d), fixed)],
        out_specs=[pl.BlockSpec((tm, d), row), pl.BlockSpec((tm, d), row)],
        out_shape=[jax.ShapeDtypeStruct((t, d), F32), jax.ShapeDtypeStruct((t, d), BF16)],
        scratch_shapes=[pltpu.VMEM((tm + 2 * halo, d), F32), pltpu.VMEM((tm, d), F32)],
        compiler_params=_params("parallel"),
        name="pool_mixer",
    )(x, x, x, c_w, c_b, c_scale, ln_g, ln_b)


def _conv_kernel(h_ref, prev_ref, next_ref, x_ref, wdw_ref, bdw_ref, cg_ref, cb_ref, wo_ref, bo_ref, g_ref, b_ref,
                 xo_ref, xb_ref, ext_ref, conv_ref, *, alpha, t1, n1, n2, rows, cols):
    tm, d = h_ref.shape
    halo = prev_ref.shape[0]
    width = wdw_ref.shape[0] // SUBLANES_F32
    n, pos0 = _seq_pos(pl.program_id(0) * tm, t1, n1, n2)
    ext_ref[0, 0:halo, :] = jnp.where(pos0 > 0, prev_ref[...].astype(F32), 0.0)
    ext_ref[0, halo:halo + tm, :] = h_ref[...].astype(F32)
    ext_ref[0, halo + tm:halo + tm + halo, :] = jnp.where(pos0 + tm < n, next_ref[...].astype(F32), 0.0)
    span = tm + 2 * halo - SUBLANES_F32
    for s in range(1, SUBLANES_F32):
        ext_ref[s, 0:span, :] = ext_ref[0, s:s + span, :]
    first = halo - width // 2

    sub = SUBLANES_F32

    def chunk(r, carry):
        r0 = pl.multiple_of(r * rows, rows)
        for c0 in range(0, d, cols):
            acc = [None] * (rows // sub)
            for k in range(width):
                s = (first + k) % sub
                w = wdw_ref[k * sub:(k + 1) * sub, c0:c0 + cols]
                for g in range(rows // sub):
                    tap = ext_ref[s, pl.ds(r0 + (first + k - s) + g * sub, sub), c0:c0 + cols] * w
                    acc[g] = tap if acc[g] is None else acc[g] + tap
            for g in range(rows // sub):
                conv_ref[pl.ds(r0 + g * sub, sub), c0:c0 + cols] = acc[g]
        return carry

    lax.fori_loop(0, tm // rows, chunk, 0)
    c = _silu(_ln_rows(conv_ref[...] + bdw_ref[...], cg_ref[...], cb_ref[...])).astype(BF16)
    h = jnp.dot(c, wo_ref[...], preferred_element_type=F32) + bo_ref[...]
    _store_res_ln(alpha * x_ref[...] + h, g_ref, b_ref, xo_ref, xb_ref)


def _conv_mixer(hg, x, w_dw, b_dw, c_g, c_b, w_out, b_out, ln_g, ln_b, alpha, t1, n1, n2):
    t, d = x.shape
    halo = SUBLANES_BF16
    assert w_dw.shape[0] // 2 <= halo
    w_dw = jnp.repeat(w_dw, SUBLANES_F32, axis=0)
    tm = _tile(math.gcd(n1, n2), 512)
    hb = tm // halo
    last = t // halo - 1
    row = lambda i: (i, 0)
    fixed = lambda i: (0, 0)
    vec = pl.BlockSpec((1, d), fixed)
    return pl.pallas_call(
        functools.partial(_conv_kernel, alpha=alpha, t1=t1, n1=n1, n2=n2,
                          rows=4 * SUBLANES_F32, cols=4 * LANES),
        grid=(t // tm,),
        in_specs=[pl.BlockSpec((tm, d), row),
                  pl.BlockSpec((halo, d), lambda i: (jnp.maximum(i * hb - 1, 0), 0)),
                  pl.BlockSpec((halo, d), lambda i: (jnp.minimum((i + 1) * hb, last), 0)),
                  pl.BlockSpec((tm, d), row),
                  pl.BlockSpec(w_dw.shape, fixed), vec, vec, vec,
                  pl.BlockSpec((d, d), fixed), vec, vec, vec],
        out_specs=[pl.BlockSpec((tm, d), row), pl.BlockSpec((tm, d), row)],
        out_shape=[jax.ShapeDtypeStruct((t, d), F32), jax.ShapeDtypeStruct((t, d), BF16)],
        scratch_shapes=[pltpu.VMEM((SUBLANES_F32, tm + 2 * halo, d), F32), pltpu.VMEM((tm, d), F32)],
        compiler_params=_params("parallel"),
        name="conv_mixer",
    )(hg, hg, hg, x, w_dw, b_dw, c_g, c_b, w_out, b_out, ln_g, ln_b)


def _router_kernel(x_ref, w_ref, b_ref, tri_ref, idx_ref, gate_ref, cnt_ref, run_ref, *, n_experts):
    @pl.when(pl.program_id(0) == 0)
    def _():
        run_ref[...] = jnp.zeros_like(run_ref)

    x = x_ref[...]
    w = w_ref[...]
    x_hi = x.astype(BF16)
    x_lo = (x - x_hi.astype(F32)).astype(BF16)
    w_hi = w.astype(BF16)
    w_lo = (w - w_hi.astype(F32)).astype(BF16)
    logits = (jnp.dot(x_hi, w_hi, preferred_element_type=F32)
              + (jnp.dot(x_hi, w_lo, preferred_element_type=F32) + jnp.dot(x_lo, w_hi, preferred_element_type=F32))
              + b_ref[...])
    lane = lax.broadcasted_iota(jnp.int32, logits.shape, 1)
    neg = jnp.float32(-jnp.inf)
    logits = jnp.where(lane < n_experts, logits, neg)
    m1 = jnp.max(logits, axis=-1, keepdims=True)
    i1 = jnp.min(jnp.where(logits == m1, lane, LANES), axis=-1, keepdims=True)
    rest = jnp.where(lane == i1, neg, logits)
    m2 = jnp.max(rest, axis=-1, keepdims=True)
    i2 = jnp.min(jnp.where(rest == m2, lane, LANES), axis=-1, keepdims=True)
    e2 = jnp.exp(m2 - m1)
    den = 1.0 + e2
    gate_ref[...] = jnp.where(lane == 0, 1.0 / den, jnp.where(lane == 1, e2 / den, 0.0))
    chosen = jnp.where((lane == i1) | (lane == i2), 1.0, 0.0)
    before = jnp.dot(tri_ref[...], chosen.astype(BF16), preferred_element_type=F32) + run_ref[...]
    r1 = jnp.sum(jnp.where(lane == i1, before, 0.0), axis=-1, keepdims=True).astype(jnp.int32)
    r2 = jnp.sum(jnp.where(lane == i2, before, 0.0), axis=-1, keepdims=True).astype(jnp.int32)
    idx_ref[...] = jnp.where(lane == 0, i1, jnp.where(lane == 1, i2, jnp.where(lane == 2, r1, jnp.where(lane == 3, r2, 0))))
    run = run_ref[...] + jnp.sum(chosen, axis=0, keepdims=True)
    run_ref[...] = run
    cnt_ref[...] = run.astype(jnp.int32)


def _router(x, w_r, b_r):
    t, d = x.shape
    n_experts = w_r.shape[1]
    w_pad = jnp.zeros((d, LANES), F32).at[:, :n_experts].set(w_r)
    b_pad = jnp.zeros((1, LANES), F32).at[0, :n_experts].set(b_r)
    tm = _tile(t, 1024)
    tri = (jnp.arange(tm)[:, None] > jnp.arange(tm)[None, :]).astype(BF16)
    row = lambda i: (i, 0)
    fixed = lambda i: (0, 0)
    return pl.pallas_call(
        functools.partial(_router_kernel, n_experts=n_experts),
        grid=(t // tm,),
        in_specs=[pl.BlockSpec((tm, d), row), pl.BlockSpec((d, LANES), fixed), pl.BlockSpec((1, LANES), fixed),
                  pl.BlockSpec((tm, tm), fixed)],
        out_specs=[pl.BlockSpec((tm, LANES), row), pl.BlockSpec((tm, LANES), row), pl.BlockSpec((1, LANES), fixed)],
        out_shape=[jax.ShapeDtypeStruct((t, LANES), jnp.int32), jax.ShapeDtypeStruct((t, LANES), F32),
                   jax.ShapeDtypeStruct((1, LANES), jnp.int32)],
        scratch_shapes=[pltpu.VMEM((1, LANES), F32)],
        compiler_params=_params("arbitrary"),
        name="router",
    )(x, w_pad, b_pad, tri)


def _moe_dispatch(idx, counts, n_experts, tm):
    t = idx.shape[0]
    s = t * TOP_K
    padded = ((counts + tm - 1) // tm) * tm
    pend = jnp.cumsum(padded)
    pstart = pend - padded
    onehot = idx[:, :TOP_K, None] == jnp.arange(n_experts, dtype=jnp.int32)
    dest = jnp.sum(jnp.where(onehot, pstart, 0), axis=-1) + idx[:, TOP_K:2 * TOP_K]
    n_tiles = s // tm + n_experts
    tok = jnp.broadcast_to(jnp.arange(t, dtype=jnp.int32)[:, None], (t, TOP_K))
    src_tok = jnp.zeros((n_tiles * tm,), jnp.int32).at[dest.reshape(s)].set(tok.reshape(s))
    nv = (pend[-1] // tm).astype(jnp.int32)
    tile_idx = jnp.arange(n_tiles, dtype=jnp.int32)
    te = jnp.searchsorted(pend, jnp.minimum(tile_idx, nv - 1) * tm, side="right").astype(jnp.int32)
    return src_tok, te, nv.reshape(1), dest


def _moe_layer(x, xb, w_r, b_r, w1, w3, w2, ln_g, ln_b, alpha):
    t, d = x.shape
    n_experts = w_r.shape[1]
    tm = _tile(t * TOP_K, 1024)
    idx, gate, counts = _router(x, w_r, b_r)
    src_tok, te, nv, dest = _moe_dispatch(idx, counts[0, :n_experts], n_experts, tm)
    xs = jnp.take(xb, src_tok, axis=0)
    out_p = _ffn_moe(xs, te, nv, w1, w3, w2, tm)
    o0 = jnp.take(out_p, dest[:, 0], axis=0)
    o1 = jnp.take(out_p, dest[:, 1], axis=0)
    return _add_res_ln(x, o0, o1, gate, ln_g, ln_b, alpha)


def kernel(x_prompt, x_sample, ln_mix_g, ln_mix_b, ln_ffn_g, ln_ffn_b, a_w_in, a_b_in, a_ln_g, a_ln_b, a_w_s, a_b_s, a_w_out, b_w_qkv, b_q_g, b_k_g, b_w_o, c_w, c_b, c_scale, d_w_in, d_b_in, d_w_dw, d_b_dw, d_ln_g, d_ln_b, d_w_out, d_b_out, f_w1, f_w3, f_w2, m_w_r, m_b_r, m_w1, m_w3, m_w2):
    bsz1, n1, d = x_prompt.shape
    bsz2, n2, _ = x_sample.shape
    t1, t2 = bsz1 * n1, bsz2 * n2
    depth = ln_mix_g.shape[0]
    alpha = (2.0 * depth) ** 0.25
    vec = lambda v: v.reshape(1, -1).astype(F32)
    bf = lambda w: w.astype(BF16)

    x = jnp.concatenate([x_prompt.reshape(t1, d), x_sample.reshape(t2, d)], axis=0)
    xb = x.astype(BF16)
    for l in range(depth):
        kind, i = l % 4, l // 4
        mg, mb = vec(ln_mix_g[l]), vec(ln_mix_b[l])
        if kind == 0:
            z = _mm_gelu(xb, bf(a_w_in[i]), vec(a_b_in[i]))
            x, xb = _gmlp_gate(z, x, vec(a_ln_g[i]), vec(a_ln_b[i]), bf(a_w_s[i]), a_b_s[i].T.astype(F32),
                               bf(a_w_out[i]), mg, mb, alpha)
        elif kind == 1:
            two_heads = lambda g: jnp.tile(g.astype(F32), 2).reshape(1, LANES)
            q, kt, va = _qkv(xb, bf(b_w_qkv[i]), two_heads(b_q_g[i]), two_heads(b_k_g[i]), t1, n1, n2)
            o = _attention_group(q, kt, va, None, 0, bsz1, n1, d)
            o = _attention_group(q, kt, va, o, t1 // n2, bsz2, n2, d)
            x, xb = _mm_res_ln(o, bf(b_w_o[i]), x, mg, mb, alpha)
        elif kind == 2:
            x, xb = _pool_mixer(x, bf(c_w[i]), vec(c_b[i]), vec(c_scale[i]), mg, mb, alpha, t1, n1, n2)
        else:
            hg = _mm_glu(xb, bf(d_w_in[i]), vec(d_b_in[i]))
            x, xb = _conv_mixer(hg, x, d_w_dw[i].astype(F32), vec(d_b_dw[i]), vec(d_ln_g[i]), vec(d_ln_b[i]),
                                bf(d_w_out[i]), vec(d_b_out[i]), mg, mb, alpha, t1, n1, n2)
        fg, fb = vec(ln_ffn_g[l]), vec(ln_ffn_b[l])
        j = l // 2
        if l % 2 == 0:
            x, xb = _ffn_dense(xb, x, bf(f_w1[j:j + 1]), bf(f_w3[j:j + 1]), bf(f_w2[j:j + 1]), fg, fb, alpha)
        else:
            x, xb = _moe_layer(x, xb, m_w_r[j], m_b_r[j], bf(m_w1[j]), bf(m_w3[j]), bf(m_w2[j]), fg, fb, alpha)
    return x[:t1].reshape(bsz1, n1, d), x[t1:].reshape(bsz2, n2, d)
```

```python
import functools
import math

import jax
import jax.numpy as jnp
from jax import lax
from jax.experimental import pallas as pl
from jax.experimental.pallas import tpu as pltpu

F32 = jnp.float32
BF16 = jnp.bfloat16

GRID_W = 64
BLOCK = 128
A_GROUPS = 8
HEAD_DIM = 64
N_KV_HEADS = 4
ROPE_THETA = 10000.0
POOL_WINDOWS = (2, 4, 8, 16)
TOP_K = 2
LN_EPS = 1e-5
RMS_EPS = 1e-6

LANES = 128
SUBLANES_F32 = 8
SUBLANES_BF16 = 16
VMEM_LIMIT = 56 * 1024 * 1024


def _params(*sem):
    return pltpu.CompilerParams(dimension_semantics=sem, vmem_limit_bytes=VMEM_LIMIT)


def _tile(total, target):
    t = math.gcd(total, target)
    assert t % SUBLANES_BF16 == 0, (total, target)
    return t


def _ln_rows(y, g, b):
    mu = jnp.mean(y, axis=-1, keepdims=True)
    yc = y - mu
    var = jnp.mean(yc * yc, axis=-1, keepdims=True)
    return yc * lax.rsqrt(var + LN_EPS) * g + b


def _gelu_tanh(x):
    c = math.sqrt(2.0 / math.pi)
    return x * (0.5 * (1.0 + jnp.tanh(c * (x + 0.044715 * (x * x * x)))))


def _silu(x):
    return x * jax.nn.sigmoid(x)


def _cast_kernel(x_ref, o_ref):
    o_ref[...] = x_ref[...].astype(o_ref.dtype)


def _to_bf16(w):
    cols = w.shape[-1]
    rows = w.size // cols
    tm = _tile(rows, 256)
    out = pl.pallas_call(
        _cast_kernel,
        grid=(rows // tm,),
        in_specs=[pl.BlockSpec((tm, cols), lambda i: (i, 0))],
        out_specs=pl.BlockSpec((tm, cols), lambda i: (i, 0)),
        out_shape=jax.ShapeDtypeStruct((rows, cols), BF16),
        compiler_params=_params("parallel"),
        name="to_bf16",
    )(w.reshape(rows, cols))
    return out.reshape(w.shape)


def _store_res_ln(y, g_ref, b_ref, xo_ref, xb_ref):
    out = _ln_rows(y, g_ref[...], b_ref[...])
    xo_ref[...] = out
    xb_ref[...] = out.astype(BF16)


def _mm_gelu_kernel(x_ref, w_ref, b_ref, o_ref):
    z = jnp.dot(x_ref[...], w_ref[...], preferred_element_type=F32) + b_ref[...]
    o_ref[...] = _gelu_tanh(z).astype(o_ref.dtype)


def _mm_gelu(xb, w, b):
    t, k = xb.shape
    n = w.shape[1]
    tm, tn = _tile(t, 1024), _tile(n, 1024)
    return pl.pallas_call(
        _mm_gelu_kernel,
        grid=(n // tn, t // tm),
        in_specs=[pl.BlockSpec((tm, k), lambda j, i: (i, 0)),
                  pl.BlockSpec((k, tn), lambda j, i: (0, j)),
                  pl.BlockSpec((1, tn), lambda j, i: (0, j))],
        out_specs=pl.BlockSpec((tm, tn), lambda j, i: (i, j)),
        out_shape=jax.ShapeDtypeStruct((t, n), BF16),
        compiler_params=_params("parallel", "parallel"),
        name="mm_gelu",
    )(xb, w, b)


def _mm_glu_kernel(x_ref, wa_ref, wg_ref, ba_ref, bg_ref, o_ref):
    x = x_ref[...]
    a = jnp.dot(x, wa_ref[...], preferred_element_type=F32) + ba_ref[...]
    g = jnp.dot(x, wg_ref[...], preferred_element_type=F32) + bg_ref[...]
    o_ref[...] = (a * jax.nn.sigmoid(g)).astype(o_ref.dtype)


def _mm_glu(xb, w, b):
    t, k = xb.shape
    n = w.shape[1] // 2
    tm, tn = _tile(t, 1024), _tile(n, 512)
    nj = n // tn
    return pl.pallas_call(
        _mm_glu_kernel,
        grid=(nj, t // tm),
        in_specs=[pl.BlockSpec((tm, k), lambda j, i: (i, 0)),
                  pl.BlockSpec((k, tn), lambda j, i: (0, j)),
                  pl.BlockSpec((k, tn), lambda j, i: (0, j + nj)),
                  pl.BlockSpec((1, tn), lambda j, i: (0, j)),
                  pl.BlockSpec((1, tn), lambda j, i: (0, j + nj))],
        out_specs=pl.BlockSpec((tm, tn), lambda j, i: (i, j)),
        out_shape=jax.ShapeDtypeStruct((t, n), BF16),
        compiler_params=_params("parallel", "parallel"),
        name="mm_glu",
    )(xb, w, w, b, b)


def _gmlp_gate_kernel(z_ref, x_ref, vg_ref, vb_ref, ws_ref, bs_ref, wo_ref, g_ref, b_ref,
                      xo_ref, xb_ref, gated_ref, *, alpha, half, gdim):
    tm = z_ref.shape[0]
    v = _ln_rows(z_ref[:, half:].astype(F32), vg_ref[...], vb_ref[...]).astype(BF16)
    for c in range(tm // BLOCK):
        rows = slice(c * BLOCK, (c + 1) * BLOCK)
        for gi in range(A_GROUPS):
            cols = slice(gi * gdim, (gi + 1) * gdim)
            sv = jnp.dot(ws_ref[gi], v[rows, cols], preferred_element_type=F32) + bs_ref[:, gi:gi + 1]
            gated_ref[rows, cols] = (z_ref[rows, cols].astype(F32) * sv).astype(BF16)
    h = jnp.dot(gated_ref[...], wo_ref[...], preferred_element_type=F32)
    _store_res_ln(alpha * x_ref[...] + h, g_ref, b_ref, xo_ref, xb_ref)


def _gmlp_gate(z, x, v_g, v_b, w_s, b_s_t, w_out, ln_g, ln_b, alpha):
    t, d = x.shape
    half = z.shape[1] // 2
    tm = _tile(t, 2 * BLOCK)
    row = lambda i: (i, 0)
    fixed2 = lambda i: (0, 0)
    return pl.pallas_call(
        functools.partial(_gmlp_gate_kernel, alpha=alpha, half=half, gdim=half // A_GROUPS),
        grid=(t // tm,),
        in_specs=[pl.BlockSpec((tm, 2 * half), row),
                  pl.BlockSpec((tm, d), row),
                  pl.BlockSpec((1, half), fixed2),
                  pl.BlockSpec((1, half), fixed2),
                  pl.BlockSpec((A_GROUPS, BLOCK, BLOCK), lambda i: (0, 0, 0)),
                  pl.BlockSpec((BLOCK, A_GROUPS), fixed2),
                  pl.BlockSpec((half, d), fixed2),
                  pl.BlockSpec((1, d), fixed2),
                  pl.BlockSpec((1, d), fixed2)],
        out_specs=[pl.BlockSpec((tm, d), row), pl.BlockSpec((tm, d), row)],
        out_shape=[jax.ShapeDtypeStruct((t, d), F32), jax.ShapeDtypeStruct((t, d), BF16)],
        scratch_shapes=[pltpu.VMEM((tm, half), BF16)],
        compiler_params=_params("parallel"),
        name="gmlp_gate",
    )(z, x, v_g, v_b, w_s, b_s_t, w_out, ln_g, ln_b)


def _ffn_accumulate(i, nv_ref, x_ref, w1_ref, w3_ref, w2_ref, acc_ref):
    c = pl.program_id(1)

    @pl.when(c == 0)
    def _():
        acc_ref[...] = jnp.zeros_like(acc_ref)

    @pl.when(i < nv_ref[0])
    def _():
        x = x_ref[...]
        a = jnp.dot(x, w1_ref[0], preferred_element_type=F32)
        b = jnp.dot(x, w3_ref[0], preferred_element_type=F32)
        mid = (_silu(a) * b).astype(BF16)
        acc_ref[...] += jnp.dot(mid, w2_ref[0], preferred_element_type=F32)


def _ffn_dense_kernel(te_ref, nv_ref, x_ref, w1_ref, w3_ref, w2_ref, r_ref, g_ref, b_ref,
                      xo_ref, xb_ref, acc_ref, *, alpha):
    _ffn_accumulate(pl.program_id(0), nv_ref, x_ref, w1_ref, w3_ref, w2_ref, acc_ref)

    @pl.when(pl.program_id(1) == pl.num_programs(1) - 1)
    def _():
        _store_res_ln(alpha * r_ref[...] + acc_ref[...], g_ref, b_ref, xo_ref, xb_ref)


def _ffn_moe_kernel(te_ref, nv_ref, x_ref, w1_ref, w3_ref, w2_ref, o_ref, acc_ref):
    _ffn_accumulate(pl.program_id(0), nv_ref, x_ref, w1_ref, w3_ref, w2_ref, acc_ref)

    @pl.when(pl.program_id(1) == pl.num_programs(1) - 1)
    def _():
        o_ref[...] = acc_ref[...].astype(o_ref.dtype)


def _ffn_specs(tm, d, f, tf):
    nf = f // tf

    def chunk(i, c, te, nv):
        return jnp.where(i < nv[0], c, nf - 1)

    def row(i, c, te, nv):
        return (jnp.minimum(i, nv[0] - 1), 0)

    return [pl.BlockSpec((tm, d), row),
            pl.BlockSpec((1, d, tf), lambda i, c, te, nv: (te[i], 0, chunk(i, c, te, nv))),
            pl.BlockSpec((1, d, tf), lambda i, c, te, nv: (te[i], 0, chunk(i, c, te, nv))),
            pl.BlockSpec((1, tf, d), lambda i, c, te, nv: (te[i], chunk(i, c, te, nv), 0))], nf


def _ffn_dense(xb, x, w1, w3, w2, j, ln_g, ln_b, alpha):
    t, d = x.shape
    f = w1.shape[2]
    tm, tf = _tile(t, 1024), _tile(f, 512)
    specs, nf = _ffn_specs(tm, d, f, tf)
    row = lambda i, c, te, nv: (i, 0)
    fixed = lambda i, c, te, nv: (0, 0)
    te = jnp.full((t // tm,), j, jnp.int32)
    nv = jnp.full((1,), t // tm, jnp.int32)
    return pl.pallas_call(
        functools.partial(_ffn_dense_kernel, alpha=alpha),
        grid_spec=pltpu.PrefetchScalarGridSpec(
            num_scalar_prefetch=2, grid=(t // tm, nf),
            in_specs=specs + [pl.BlockSpec((tm, d), row), pl.BlockSpec((1, d), fixed), pl.BlockSpec((1, d), fixed)],
            out_specs=[pl.BlockSpec((tm, d), row), pl.BlockSpec((tm, d), row)],
            scratch_shapes=[pltpu.VMEM((tm, d), F32)]),
        out_shape=[jax.ShapeDtypeStruct((t, d), F32), jax.ShapeDtypeStruct((t, d), BF16)],
        compiler_params=_params("parallel", "arbitrary"),
        name="ffn_dense",
    )(te, nv, xb, w1, w3, w2, x, ln_g, ln_b)


def _ffn_moe(xs, te, nv, w1, w3, w2, tm):
    p, d = xs.shape
    f = w1.shape[2]
    tf = _tile(f, 512)
    specs, nf = _ffn_specs(tm, d, f, tf)
    row = lambda i, c, te, nv: (i, 0)
    return pl.pallas_call(
        _ffn_moe_kernel,
        grid_spec=pltpu.PrefetchScalarGridSpec(
            num_scalar_prefetch=2, grid=(p // tm, nf),
            in_specs=specs,
            out_specs=pl.BlockSpec((tm, d), row),
            scratch_shapes=[pltpu.VMEM((tm, d), F32)]),
        out_shape=jax.ShapeDtypeStruct((p, d), BF16),
        compiler_params=_params("parallel", "arbitrary"),
        name="ffn_moe",
    )(te, nv, xs, w1, w3, w2)


def _mm_res_ln_kernel(a_ref, w_ref, r_ref, g_ref, b_ref, xo_ref, xb_ref, *, alpha):
    h = jnp.dot(a_ref[...], w_ref[...], preferred_element_type=F32)
    _store_res_ln(alpha * r_ref[...] + h, g_ref, b_ref, xo_ref, xb_ref)


def _mm_res_ln(a, w, x, ln_g, ln_b, alpha):
    t, d = x.shape
    k = a.shape[1]
    tm = _tile(t, 512)
    row = lambda i: (i, 0)
    fixed = lambda i: (0, 0)
    return pl.pallas_call(
        functools.partial(_mm_res_ln_kernel, alpha=alpha),
        grid=(t // tm,),
        in_specs=[pl.BlockSpec((tm, k), row), pl.BlockSpec((k, d), fixed), pl.BlockSpec((tm, d), row),
                  pl.BlockSpec((1, d), fixed), pl.BlockSpec((1, d), fixed)],
        out_specs=[pl.BlockSpec((tm, d), row), pl.BlockSpec((tm, d), row)],
        out_shape=[jax.ShapeDtypeStruct((t, d), F32), jax.ShapeDtypeStruct((t, d), BF16)],
        compiler_params=_params("parallel"),
        name="mm_res_ln",
    )(a, w, x, ln_g, ln_b)


def _add_res_ln_kernel(r_ref, o0_ref, o1_ref, gate_ref, g_ref, b_ref, xo_ref, xb_ref, *, alpha):
    h = gate_ref[:, 0:1] * o0_ref[...].astype(F32) + gate_ref[:, 1:2] * o1_ref[...].astype(F32)
    _store_res_ln(alpha * r_ref[...] + h, g_ref, b_ref, xo_ref, xb_ref)


def _add_res_ln(x, o0, o1, gate, ln_g, ln_b, alpha):
    t, d = x.shape
    tm = _tile(t, 1024)
    row = lambda i: (i, 0)
    fixed = lambda i: (0, 0)
    return pl.pallas_call(
        functools.partial(_add_res_ln_kernel, alpha=alpha),
        grid=(t // tm,),
        in_specs=[pl.BlockSpec((tm, d), row), pl.BlockSpec((tm, d), row), pl.BlockSpec((tm, d), row),
                  pl.BlockSpec((tm, LANES), row), pl.BlockSpec((1, d), fixed), pl.BlockSpec((1, d), fixed)],
        out_specs=[pl.BlockSpec((tm, d), row), pl.BlockSpec((tm, d), row)],
        out_shape=[jax.ShapeDtypeStruct((t, d), F32), jax.ShapeDtypeStruct((t, d), BF16)],
        compiler_params=_params("parallel"),
        name="add_res_ln",
    )(x, o0, o1, gate, ln_g, ln_b)


def _qkv_kernel(x_ref, w_ref, qg_ref, kg_ref, cos_ref, sa_ref, sb_ref, q_ref, kt_ref, v_ref, *, n_q):
    tm = x_ref.shape[0]
    acc = jnp.dot(x_ref[...], w_ref[...], preferred_element_type=F32)
    lane = lax.broadcasted_iota(jnp.int32, (tm, LANES), 1)
    left = lane < HEAD_DIM
    cos, sin_a, sin_b = cos_ref[...], sa_ref[...], sb_ref[...]

    def norm_rope(c, g, scale):
        s = c * c
        ss_l = jnp.sum(jnp.where(left, s, 0.0), axis=-1, keepdims=True)
        ss_r = jnp.sum(jnp.where(left, 0.0, s), axis=-1, keepdims=True)
        r = lax.rsqrt(jnp.where(left, ss_l, ss_r) * (1.0 / HEAD_DIM) + RMS_EPS)
        cn = c * r * g
        y = cn * cos + pltpu.roll(cn, LANES - HEAD_DIM // 4, 1) * sin_a + pltpu.roll(cn, HEAD_DIM // 4, 1) * sin_b
        return y * scale

    def split_heads(y, fill):
        return jnp.where(left, y, fill), jnp.where(left, pltpu.roll(y, HEAD_DIM, 1), fill)

    zero = jnp.zeros((tm, LANES), F32)
    for j in range(n_q // 2):
        y = norm_rope(acc[:, j * LANES:(j + 1) * LANES], qg_ref[...], HEAD_DIM ** -0.5)
        even, odd = split_heads(y, zero)
        q_ref[:, (2 * j) * LANES:(2 * j + 1) * LANES] = even.astype(BF16)
        q_ref[:, (2 * j + 1) * LANES:(2 * j + 2) * LANES] = odd.astype(BF16)
    k_off = n_q * HEAD_DIM
    v_off = k_off + N_KV_HEADS * HEAD_DIM
    ones_col = jnp.where(lane == HEAD_DIM, 1.0, 0.0)
    for j in range(N_KV_HEADS // 2):
        y = norm_rope(acc[:, k_off + j * LANES:k_off + (j + 1) * LANES], kg_ref[...], 1.0)
        even, odd = split_heads(y, zero)
        kt_ref[(2 * j) * LANES:(2 * j + 1) * LANES, :] = even.T.astype(BF16)
        kt_ref[(2 * j + 1) * LANES:(2 * j + 2) * LANES, :] = odd.T.astype(BF16)
        even, odd = split_heads(acc[:, v_off + j * LANES:v_off + (j + 1) * LANES], ones_col)
        v_ref[:, (2 * j) * LANES:(2 * j + 1) * LANES] = even.astype(BF16)
        v_ref[:, (2 * j + 1) * LANES:(2 * j + 2) * LANES] = odd.astype(BF16)


def _rope_tables(n):
    pos = jnp.arange(n)
    row = (pos // GRID_W).astype(F32)
    col = (pos % GRID_W).astype(F32)
    sec = HEAD_DIM // 2
    inv = ROPE_THETA ** (-jnp.arange(0, sec, 2, dtype=F32) / sec)
    ang_r = row[:, None] * inv[None, :]
    ang_c = col[:, None] * inv[None, :]
    ang = jnp.concatenate([ang_r, ang_r, ang_c, ang_c], axis=-1)
    ang = jnp.concatenate([ang, ang], axis=-1)
    first = (jnp.arange(LANES) % (HEAD_DIM // 2)) < HEAD_DIM // 4
    sin = jnp.sin(ang)
    return jnp.cos(ang), jnp.where(first, -sin, 0.0), jnp.where(first, 0.0, sin)


def _qkv(xb, w_qkv, q_g2, k_g2, n):
    t, d = xb.shape
    n_q = d // HEAD_DIM
    tm = _tile(n, 512)
    cos, sin_a, sin_b = _rope_tables(n)
    tab = lambda i: (lax.rem(i, n // tm), 0)
    row = lambda i: (i, 0)
    fixed = lambda i: (0, 0)
    return pl.pallas_call(
        functools.partial(_qkv_kernel, n_q=n_q),
        grid=(t // tm,),
        in_specs=[pl.BlockSpec((tm, d), row), pl.BlockSpec(w_qkv.shape, fixed),
                  pl.BlockSpec((1, LANES), fixed), pl.BlockSpec((1, LANES), fixed),
                  pl.BlockSpec((tm, LANES), tab), pl.BlockSpec((tm, LANES), tab), pl.BlockSpec((tm, LANES), tab)],
        out_specs=[pl.BlockSpec((tm, n_q * LANES), row),
                   pl.BlockSpec((N_KV_HEADS * LANES, tm), lambda i: (0, i)),
                   pl.BlockSpec((tm, N_KV_HEADS * LANES), row)],
        out_shape=[jax.ShapeDtypeStruct((t, n_q * LANES), BF16),
                   jax.ShapeDtypeStruct((N_KV_HEADS * LANES, t), BF16),
                   jax.ShapeDtypeStruct((t, N_KV_HEADS * LANES), BF16)],
        compiler_params=_params("parallel"),
        name="qkv_rope",
    )(xb, w_qkv, q_g2, k_g2, cos, sin_a, sin_b)


def _attn_kernel(q_ref, kt_ref, v_ref, o_ref, *, q_per_kv):
    tq = q_ref.shape[0]
    kt = kt_ref[...]
    va = v_ref[...]
    left = lax.broadcasted_iota(jnp.int32, (tq, LANES), 1) < HEAD_DIM
    outs = []
    for h in range(q_per_kv):
        s = jnp.dot(q_ref[:, h * LANES:(h + 1) * LANES], kt, preferred_element_type=F32)
        m = jnp.max(s, axis=-1, keepdims=True)
        p = jnp.exp(s - m).astype(BF16)
        o = jnp.dot(p, va, preferred_element_type=F32)
        outs.append(o * (1.0 / o[:, HEAD_DIM:HEAD_DIM + 1]))
    for j in range(q_per_kv // 2):
        pair = jnp.where(left, outs[2 * j], pltpu.roll(outs[2 * j + 1], HEAD_DIM, 1))
        o_ref[:, j * LANES:(j + 1) * LANES] = pair.astype(BF16)


def _attention(q, kt, va, n, d):
    t = q.shape[0]
    q_per_kv = d // HEAD_DIM // N_KV_HEADS
    tq = _tile(n, 256)
    nq = n // tq
    return pl.pallas_call(
        functools.partial(_attn_kernel, q_per_kv=q_per_kv),
        grid=(t // n, N_KV_HEADS, nq),
        in_specs=[pl.BlockSpec((tq, q_per_kv * LANES), lambda b, h, i: (b * nq + i, h)),
                  pl.BlockSpec((LANES, n), lambda b, h, i: (h, b)),
                  pl.BlockSpec((n, LANES), lambda b, h, i: (b, h))],
        out_specs=pl.BlockSpec((tq, q_per_kv * HEAD_DIM), lambda b, h, i: (b * nq + i, h)),
        out_shape=jax.ShapeDtypeStruct((t, d), BF16),
        compiler_params=_params("parallel", "parallel", "arbitrary"),
        name="attention",
    )(q, kt, va)


def _pool_kernel(x_ref, prev_ref, next_ref, w_ref, cb_ref, cs_ref, g_ref, b_ref, xo_ref, xb_ref,
                 ext_ref, h_ref, *, alpha, n):
    tm, d = x_ref.shape
    halo = prev_ref.shape[0]
    gdim = d // len(POOL_WINDOWS)
    pos0 = lax.rem(pl.program_id(0) * tm, n)
    x = x_ref[...]
    ext_ref[0:halo, :] = jnp.where(pos0 > 0, prev_ref[...], 0.0)
    ext_ref[halo:halo + tm, :] = x
    ext_ref[halo + tm:halo + tm + halo, :] = jnp.where(pos0 + tm < n, next_ref[...], 0.0)
    pos = pos0 + lax.broadcasted_iota(jnp.int32, (tm, 1), 0)
    for gi, w in enumerate(POOL_WINDOWS):
        cols = slice(gi * gdim, (gi + 1) * gdim)
        wsum = ext_ref[halo - w // 2:halo - w // 2 + tm, cols]
        for j in range(1 - w // 2, w - w // 2):
            wsum = wsum + ext_ref[halo + j:halo + j + tm, cols]
        cnt = (jnp.minimum(pos + (w - w // 2), n) - jnp.maximum(pos - w // 2, 0)).astype(F32)
        mix = (wsum * (1.0 / cnt) - x[:, cols]).astype(BF16)
        y = jnp.dot(mix, w_ref[gi], preferred_element_type=F32) + cb_ref[:, cols]
        h_ref[:, cols] = y * cs_ref[:, cols]
    _store_res_ln(alpha * x + h_ref[...], g_ref, b_ref, xo_ref, xb_ref)


def _pool_mixer(x, c_w, c_b, c_scale, ln_g, ln_b, alpha, n):
    t, d = x.shape
    halo = SUBLANES_F32
    assert max(POOL_WINDOWS) // 2 <= halo
    tm = _tile(n, 512)
    hb = tm // halo
    last = t // halo - 1
    row = lambda i: (i, 0)
    fixed = lambda i: (0, 0)
    gdim = d // len(POOL_WINDOWS)
    return pl.pallas_call(
        functools.partial(_pool_kernel, alpha=alpha, n=n),
        grid=(t // tm,),
        in_specs=[pl.BlockSpec((tm, d), row),
                  pl.BlockSpec((halo, d), lambda i: (jnp.maximum(i * hb - 1, 0), 0)),
                  pl.BlockSpec((halo, d), lambda i: (jnp.minimum((i + 1) * hb, last), 0)),
                  pl.BlockSpec((len(POOL_WINDOWS), gdim, gdim), lambda i: (0, 0, 0)),
                  pl.BlockSpec((1, d), fixed), pl.BlockSpec((1, d), fixed),
                  pl.BlockSpec((1, d), fixed), pl.BlockSpec((1, d), fixed)],
        out_specs=[pl.BlockSpec((tm, d), row), pl.BlockSpec((tm, d), row)],
        out_shape=[jax.ShapeDtypeStruct((t, d), F32), jax.ShapeDtypeStruct((t, d), BF16)],
        scratch_shapes=[pltpu.VMEM((tm + 2 * halo, d), F32), pltpu.VMEM((tm, d), F32)],
        compiler_params=_params("parallel"),
        name="pool_mixer",
    )(x, x, x, c_w, c_b, c_scale, ln_g, ln_b)


def _conv_kernel(h_ref, prev_ref, next_ref, x_ref, wdw_ref, bdw_ref, cg_ref, cb_ref, wo_ref, bo_ref, g_ref, b_ref,
                 xo_ref, xb_ref, ext_ref, conv_ref, *, alpha, n, rows, cols):
    tm, d = h_ref.shape
    halo = prev_ref.shape[0]
    width = wdw_ref.shape[0] // SUBLANES_F32
    pos0 = lax.rem(pl.program_id(0) * tm, n)
    ext_ref[0, 0:halo, :] = jnp.where(pos0 > 0, prev_ref[...].astype(F32), 0.0)
    ext_ref[0, halo:halo + tm, :] = h_ref[...].astype(F32)
    ext_ref[0, halo + tm:halo + tm + halo, :] = jnp.where(pos0 + tm < n, next_ref[...].astype(F32), 0.0)
    span = tm + 2 * halo - SUBLANES_F32
    for s in range(1, SUBLANES_F32):
        ext_ref[s, 0:span, :] = ext_ref[0, s:s + span, :]
    first = halo - width // 2

    sub = SUBLANES_F32

    def chunk(r, carry):
        r0 = pl.multiple_of(r * rows, rows)
        for c0 in range(0, d, cols):
            acc = [None] * (rows // sub)
            for k in range(width):
                s = (first + k) % sub
                w = wdw_ref[k * sub:(k + 1) * sub, c0:c0 + cols]
                for g in range(rows // sub):
                    tap = ext_ref[s, pl.ds(r0 + (first + k - s) + g * sub, sub), c0:c0 + cols] * w
                    acc[g] = tap if acc[g] is None else acc[g] + tap
            for g in range(rows // sub):
                conv_ref[pl.ds(r0 + g * sub, sub), c0:c0 + cols] = acc[g]
        return carry

    lax.fori_loop(0, tm // rows, chunk, 0)
    c = _silu(_ln_rows(conv_ref[...] + bdw_ref[...], cg_ref[...], cb_ref[...])).astype(BF16)
    h = jnp.dot(c, wo_ref[...], preferred_element_type=F32) + bo_ref[...]
    _store_res_ln(alpha * x_ref[...] + h, g_ref, b_ref, xo_ref, xb_ref)


def _conv_mixer(hg, x, w_dw, b_dw, c_g, c_b, w_out, b_out, ln_g, ln_b, alpha, n):
    t, d = x.shape
    halo = SUBLANES_BF16
    assert w_dw.shape[0] // SUBLANES_F32 // 2 <= halo
    tm = _tile(n, 512)
    hb = tm // halo
    last = t // halo - 1
    row = lambda i: (i, 0)
    fixed = lambda i: (0, 0)
    vec = pl.BlockSpec((1, d), fixed)
    return pl.pallas_call(
        functools.partial(_conv_kernel, alpha=alpha, n=n, rows=4 * SUBLANES_F32, cols=4 * LANES),
        grid=(t // tm,),
        in_specs=[pl.BlockSpec((tm, d), row),
                  pl.BlockSpec((halo, d), lambda i: (jnp.maximum(i * hb - 1, 0), 0)),
                  pl.BlockSpec((halo, d), lambda i: (jnp.minimum((i + 1) * hb, last), 0)),
                  pl.BlockSpec((tm, d), row),
                  pl.BlockSpec(w_dw.shape, fixed), vec, vec, vec,
                  pl.BlockSpec((d, d), fixed), vec, vec, vec],
        out_specs=[pl.BlockSpec((tm, d), row), pl.BlockSpec((tm, d), row)],
        out_shape=[jax.ShapeDtypeStruct((t, d), F32), jax.ShapeDtypeStruct((t, d), BF16)],
        scratch_shapes=[pltpu.VMEM((SUBLANES_F32, tm + 2 * halo, d), F32), pltpu.VMEM((tm, d), F32)],
        compiler_params=_params("parallel"),
        name="conv_mixer",
    )(hg, hg, hg, x, w_dw, b_dw, c_g, c_b, w_out, b_out, ln_g, ln_b)


def _router_kernel(x_ref, w_ref, b_ref, tri_ref, idx_ref, gate_ref, cnt_ref, run_ref, *, n_experts):
    @pl.when(pl.program_id(0) == 0)
    def _():
        run_ref[...] = jnp.zeros_like(run_ref)

    x = x_ref[...]
    w = w_ref[...]
    x_hi = x.astype(BF16)
    x_lo = (x - x_hi.astype(F32)).astype(BF16)
    w_hi = w.astype(BF16)
    w_lo = (w - w_hi.astype(F32)).astype(BF16)
    logits = (jnp.dot(x_hi, w_hi, preferred_element_type=F32)
              + (jnp.dot(x_hi, w_lo, preferred_element_type=F32) + jnp.dot(x_lo, w_hi, preferred_element_type=F32))
              + b_ref[...])
    lane = lax.broadcasted_iota(jnp.int32, logits.shape, 1)
    neg = jnp.float32(-jnp.inf)
    logits = jnp.where(lane < n_experts, logits, neg)
    m1 = jnp.max(logits, axis=-1, keepdims=True)
    i1 = jnp.min(jnp.where(logits == m1, lane, LANES), axis=-1, keepdims=True)
    rest = jnp.where(lane == i1, neg, logits)
    m2 = jnp.max(rest, axis=-1, keepdims=True)
    i2 = jnp.min(jnp.where(rest == m2, lane, LANES), axis=-1, keepdims=True)
    e2 = jnp.exp(m2 - m1)
    den = 1.0 + e2
    gate_ref[...] = jnp.where(lane == 0, 1.0 / den, jnp.where(lane == 1, e2 / den, 0.0))
    chosen = jnp.where((lane == i1) | (lane == i2), 1.0, 0.0)
    before = jnp.dot(tri_ref[...], chosen.astype(BF16), preferred_element_type=F32) + run_ref[...]
    r1 = jnp.sum(jnp.where(lane == i1, before, 0.0), axis=-1, keepdims=True).astype(jnp.int32)
    r2 = jnp.sum(jnp.where(lane == i2, before, 0.0), axis=-1, keepdims=True).astype(jnp.int32)
    idx_ref[...] = jnp.where(lane == 0, i1, jnp.where(lane == 1, i2, jnp.where(lane == 2, r1, jnp.where(lane == 3, r2, 0))))
    run = run_ref[...] + jnp.sum(chosen, axis=0, keepdims=True)
    run_ref[...] = run
    cnt_ref[...] = run.astype(jnp.int32)


def _router(x, w_r, b_r):
    t, d = x.shape
    n_experts = w_r.shape[1]
    w_pad = jnp.zeros((d, LANES), F32).at[:, :n_experts].set(w_r)
    b_pad = jnp.zeros((1, LANES), F32).at[0, :n_experts].set(b_r)
    tm = _tile(t, 1024)
    tri = (jnp.arange(tm)[:, None] > jnp.arange(tm)[None, :]).astype(BF16)
    row = lambda i: (i, 0)
    fixed = lambda i: (0, 0)
    return pl.pallas_call(
        functools.partial(_router_kernel, n_experts=n_experts),
        grid=(t // tm,),
        in_specs=[pl.BlockSpec((tm, d), row), pl.BlockSpec((d, LANES), fixed), pl.BlockSpec((1, LANES), fixed),
                  pl.BlockSpec((tm, tm), fixed)],
        out_specs=[pl.BlockSpec((tm, LANES), row), pl.BlockSpec((tm, LANES), row), pl.BlockSpec((1, LANES), fixed)],
        out_shape=[jax.ShapeDtypeStruct((t, LANES), jnp.int32), jax.ShapeDtypeStruct((t, LANES), F32),
                   jax.ShapeDtypeStruct((1, LANES), jnp.int32)],
        scratch_shapes=[pltpu.VMEM((1, LANES), F32)],
        compiler_params=_params("arbitrary"),
        name="router",
    )(x, w_pad, b_pad, tri)


def _moe_dispatch(idx, counts, n_experts, tm):
    t = idx.shape[0]
    s = t * TOP_K
    padded = ((counts + tm - 1) // tm) * tm
    pend = jnp.cumsum(padded)
    pstart = pend - padded
    onehot = idx[:, :TOP_K, None] == jnp.arange(n_experts, dtype=jnp.int32)
    dest = jnp.sum(jnp.where(onehot, pstart, 0), axis=-1) + idx[:, TOP_K:2 * TOP_K]
    n_tiles = s // tm + n_experts
    tok = jnp.broadcast_to(jnp.arange(t, dtype=jnp.int32)[:, None], (t, TOP_K))
    src_tok = jnp.zeros((n_tiles * tm,), jnp.int32).at[dest.reshape(s)].set(tok.reshape(s))
    nv = (pend[-1] // tm).astype(jnp.int32)
    tile_idx = jnp.arange(n_tiles, dtype=jnp.int32)
    te = jnp.searchsorted(pend, jnp.minimum(tile_idx, nv - 1) * tm, side="right").astype(jnp.int32)
    return src_tok, te, nv.reshape(1), dest


def _moe_layer(x, xb, w_r, b_r, w1, w3, w2, j, ln_g, ln_b, alpha):
    t, d = x.shape
    n_experts = w_r.shape[1]
    tm = _tile(t * TOP_K, 512)
    idx, gate, counts = _router(x, w_r, b_r)
    src_tok, te, nv, dest = _moe_dispatch(idx, counts[0, :n_experts], n_experts, tm)
    rows = lambda a, r: a.at[r].get(mode="promise_in_bounds")
    xs = rows(xb, src_tok)
    out_p = _ffn_moe(xs, te + j * n_experts, nv, w1, w3, w2, tm)
    o0 = rows(out_p, dest[:, 0])
    o1 = rows(out_p, dest[:, 1])
    return _add_res_ln(x, o0, o1, gate, ln_g, ln_b, alpha)


def kernel(x_prompt, x_sample, ln_mix_g, ln_mix_b, ln_ffn_g, ln_ffn_b, a_w_in, a_b_in, a_ln_g, a_ln_b, a_w_s, a_b_s, a_w_out, b_w_qkv, b_q_g, b_k_g, b_w_o, c_w, c_b, c_scale, d_w_in, d_b_in, d_w_dw, d_b_dw, d_ln_g, d_ln_b, d_w_out, d_b_out, f_w1, f_w3, f_w2, m_w_r, m_b_r, m_w1, m_w3, m_w2):
    d = x_prompt.shape[-1]
    depth = ln_mix_g.shape[0]
    alpha = (2.0 * depth) ** 0.25
    vec = lambda v: v.reshape(1, -1).astype(F32)
    two_heads = lambda g: jnp.tile(g.astype(F32), 2).reshape(1, LANES)

    a_w_in, a_w_out, b_w_qkv, b_w_o, d_w_in, d_w_out = map(_to_bf16, (a_w_in, a_w_out, b_w_qkv, b_w_o, d_w_in, d_w_out))
    f_w1, f_w3, f_w2 = map(_to_bf16, (f_w1, f_w3, f_w2))
    n_moe, n_experts = m_w1.shape[:2]
    m_w1, m_w3, m_w2 = (_to_bf16(w).reshape((n_moe * n_experts,) + w.shape[2:]) for w in (m_w1, m_w3, m_w2))
    a_w_s, c_w = a_w_s.astype(BF16), c_w.astype(BF16)
    d_w_dw = jnp.repeat(d_w_dw.astype(F32), SUBLANES_F32, axis=1)

    groups = []
    for xin in (x_prompt, x_sample):
        bsz, n, _ = xin.shape
        x = xin.reshape(bsz * n, d)
        groups.append([x, _to_bf16(x), n])
    for l in range(depth):
        kind, i = l % 4, l // 4
        mg, mb = vec(ln_mix_g[l]), vec(ln_mix_b[l])
        fg, fb = vec(ln_ffn_g[l]), vec(ln_ffn_b[l])
        j = l // 2
        for grp in groups:
            x, xb, n = grp
            if kind == 0:
                z = _mm_gelu(xb, a_w_in[i], vec(a_b_in[i]))
                x, xb = _gmlp_gate(z, x, vec(a_ln_g[i]), vec(a_ln_b[i]), a_w_s[i], a_b_s[i].T.astype(F32),
                                   a_w_out[i], mg, mb, alpha)
            elif kind == 1:
                q, kt, va = _qkv(xb, b_w_qkv[i], two_heads(b_q_g[i]), two_heads(b_k_g[i]), n)
                o = _attention(q, kt, va, n, d)
                x, xb = _mm_res_ln(o, b_w_o[i], x, mg, mb, alpha)
            elif kind == 2:
                x, xb = _pool_mixer(x, c_w[i], vec(c_b[i]), vec(c_scale[i]), mg, mb, alpha, n)
            else:
                hg = _mm_glu(xb, d_w_in[i], vec(d_b_in[i]))
                x, xb = _conv_mixer(hg, x, d_w_dw[i], vec(d_b_dw[i]), vec(d_ln_g[i]), vec(d_ln_b[i]),
                                    d_w_out[i], vec(d_b_out[i]), mg, mb, alpha, n)
            if l % 2 == 0:
                x, xb = _ffn_dense(xb, x, f_w1, f_w3, f_w2, j, fg, fb, alpha)
            else:
                x, xb = _moe_layer(x, xb, m_w_r[j], m_b_r[j], m_w1, m_w3, m_w2, j, fg, fb, alpha)
            grp[0], grp[1] = x, xb
    return tuple(x.reshape(xin.shape) for (x, _, _), xin in zip(groups, (x_prompt, x_sample)))
```

```python
import functools
import math

import jax
import jax.numpy as jnp
from jax import lax
from jax.experimental import pallas as pl
from jax.experimental.pallas import tpu as pltpu

F32 = jnp.float32
BF16 = jnp.bfloat16

GRID_W = 64
BLOCK = 128
A_GROUPS = 8
HEAD_DIM = 64
N_KV_HEADS = 4
ROPE_THETA = 10000.0
POOL_WINDOWS = (2, 4, 8, 16)
TOP_K = 2
LN_EPS = 1e-5
RMS_EPS = 1e-6

LANES = 128
SUBLANES_F32 = 8
SUBLANES_BF16 = 16
VMEM_LIMIT = 56 * 1024 * 1024
FFN_CHUNK = 512
ATTN_Q_TILE = 512
ATTN_Q_SUB = 256


def _params(*sem):
    return pltpu.CompilerParams(dimension_semantics=sem, vmem_limit_bytes=VMEM_LIMIT)


def _tile(total, target):
    t = math.gcd(total, target)
    assert t % SUBLANES_BF16 == 0, (total, target)
    return t


def _ln_rows(y, g, b):
    mu = jnp.mean(y, axis=-1, keepdims=True)
    yc = y - mu
    var = jnp.mean(yc * yc, axis=-1, keepdims=True)
    return yc * lax.rsqrt(var + LN_EPS) * g + b


def _gelu_tanh(x):
    c = math.sqrt(2.0 / math.pi)
    return x * (0.5 * (1.0 + jnp.tanh(c * (x + 0.044715 * (x * x * x)))))


def _silu(x):
    return x * jax.nn.sigmoid(x)


def _cast_kernel(x_ref, o_ref):
    o_ref[...] = x_ref[...].astype(o_ref.dtype)


def _to_bf16(w):
    cols = w.shape[-1]
    rows = w.size // cols
    tm = _tile(rows, 256)
    out = pl.pallas_call(
        _cast_kernel,
        grid=(rows // tm,),
        in_specs=[pl.BlockSpec((tm, cols), lambda i: (i, 0))],
        out_specs=pl.BlockSpec((tm, cols), lambda i: (i, 0)),
        out_shape=jax.ShapeDtypeStruct((rows, cols), BF16),
        compiler_params=_params("parallel"),
        name="to_bf16",
    )(w.reshape(rows, cols))
    return out.reshape(w.shape)


def _store_res_ln(y, g_ref, b_ref, xo_ref, xb_ref):
    out = _ln_rows(y, g_ref[...], b_ref[...])
    xo_ref[...] = out
    xb_ref[...] = out.astype(BF16)


def _mm_gelu_kernel(x_ref, w_ref, b_ref, o_ref, *, sub):
    x = x_ref[...]
    for c0 in range(0, o_ref.shape[1], sub):
        z = jnp.dot(x, w_ref[:, c0:c0 + sub], preferred_element_type=F32) + b_ref[:, c0:c0 + sub]
        o_ref[:, c0:c0 + sub] = _gelu_tanh(z).astype(o_ref.dtype)


def _mm_gelu(xb, w, b):
    t, k = xb.shape
    n = w.shape[1]
    tm, tn = _tile(t, 1024), _tile(n, 2048)
    return pl.pallas_call(
        functools.partial(_mm_gelu_kernel, sub=_tile(tn, 1024)),
        grid=(n // tn, t // tm),
        in_specs=[pl.BlockSpec((tm, k), lambda j, i: (i, 0)),
                  pl.BlockSpec((k, tn), lambda j, i: (0, j)),
                  pl.BlockSpec((1, tn), lambda j, i: (0, j))],
        out_specs=pl.BlockSpec((tm, tn), lambda j, i: (i, j)),
        out_shape=jax.ShapeDtypeStruct((t, n), BF16),
        compiler_params=_params("parallel", "parallel"),
        name="mm_gelu",
    )(xb, w, b)


def _mm_glu_kernel(x_ref, wa_ref, wg_ref, ba_ref, bg_ref, o_ref):
    x = x_ref[...]
    a = jnp.dot(x, wa_ref[...], preferred_element_type=F32) + ba_ref[...]
    g = jnp.dot(x, wg_ref[...], preferred_element_type=F32) + bg_ref[...]
    o_ref[...] = (a * jax.nn.sigmoid(g)).astype(o_ref.dtype)


def _mm_glu(xb, w, b):
    t, k = xb.shape
    n = w.shape[1] // 2
    tm, tn = _tile(t, 1024), _tile(n, 512)
    nj = n // tn
    return pl.pallas_call(
        _mm_glu_kernel,
        grid=(nj, t // tm),
        in_specs=[pl.BlockSpec((tm, k), lambda j, i: (i, 0)),
                  pl.BlockSpec((k, tn), lambda j, i: (0, j)),
                  pl.BlockSpec((k, tn), lambda j, i: (0, j + nj)),
                  pl.BlockSpec((1, tn), lambda j, i: (0, j)),
                  pl.BlockSpec((1, tn), lambda j, i: (0, j + nj))],
        out_specs=pl.BlockSpec((tm, tn), lambda j, i: (i, j)),
        out_shape=jax.ShapeDtypeStruct((t, n), BF16),
        compiler_params=_params("parallel", "parallel"),
        name="mm_glu",
    )(xb, w, w, b, b)


def _gmlp_gate_kernel(z_ref, x_ref, vg_ref, vb_ref, ws_ref, bs_ref, wo_ref, g_ref, b_ref,
                      xo_ref, xb_ref, gated_ref, *, alpha, half, gdim):
    tm = z_ref.shape[0]
    v = _ln_rows(z_ref[:, half:].astype(F32), vg_ref[...], vb_ref[...]).astype(BF16)
    for c in range(tm // BLOCK):
        rows = slice(c * BLOCK, (c + 1) * BLOCK)
        for gi in range(A_GROUPS):
            cols = slice(gi * gdim, (gi + 1) * gdim)
            sv = jnp.dot(ws_ref[gi], v[rows, cols], preferred_element_type=F32) + bs_ref[:, gi:gi + 1]
            gated_ref[rows, cols] = (z_ref[rows, cols].astype(F32) * sv).astype(BF16)
    h = jnp.dot(gated_ref[...], wo_ref[...], preferred_element_type=F32)
    _store_res_ln(alpha * x_ref[...] + h, g_ref, b_ref, xo_ref, xb_ref)


def _gmlp_gate(z, x, v_g, v_b, w_s, b_s_t, w_out, ln_g, ln_b, alpha):
    t, d = x.shape
    half = z.shape[1] // 2
    tm = _tile(t, 2 * BLOCK)
    row = lambda i: (i, 0)
    fixed2 = lambda i: (0, 0)
    return pl.pallas_call(
        functools.partial(_gmlp_gate_kernel, alpha=alpha, half=half, gdim=half // A_GROUPS),
        grid=(t // tm,),
        in_specs=[pl.BlockSpec((tm, 2 * half), row),
                  pl.BlockSpec((tm, d), row),
                  pl.BlockSpec((1, half), fixed2),
                  pl.BlockSpec((1, half), fixed2),
                  pl.BlockSpec((A_GROUPS, BLOCK, BLOCK), lambda i: (0, 0, 0)),
                  pl.BlockSpec((BLOCK, A_GROUPS), fixed2),
                  pl.BlockSpec((half, d), fixed2),
                  pl.BlockSpec((1, d), fixed2),
                  pl.BlockSpec((1, d), fixed2)],
        out_specs=[pl.BlockSpec((tm, d), row), pl.BlockSpec((tm, d), row)],
        out_shape=[jax.ShapeDtypeStruct((t, d), F32), jax.ShapeDtypeStruct((t, d), BF16)],
        scratch_shapes=[pltpu.VMEM((tm, half), BF16)],
        compiler_params=_params("parallel"),
        name="gmlp_gate",
    )(z, x, v_g, v_b, w_s, b_s_t, w_out, ln_g, ln_b)


def _ffn_accumulate(i, nv_ref, x_ref, w1_ref, w3_ref, w2_ref, acc_ref):
    c = pl.program_id(1)

    @pl.when(c == 0)
    def _():
        acc_ref[...] = jnp.zeros_like(acc_ref)

    @pl.when(i < nv_ref[0])
    def _():
        x = x_ref[...]
        a = jnp.dot(x, w1_ref[0], preferred_element_type=F32)
        b = jnp.dot(x, w3_ref[0], preferred_element_type=F32)
        mid = (_silu(a) * b).astype(BF16)
        acc_ref[...] += jnp.dot(mid, w2_ref[0], preferred_element_type=F32)


def _ffn_dense_kernel(te_ref, nv_ref, x_ref, w1_ref, w3_ref, w2_ref, r_ref, g_ref, b_ref,
                      xo_ref, xb_ref, acc_ref, *, alpha):
    _ffn_accumulate(pl.program_id(0), nv_ref, x_ref, w1_ref, w3_ref, w2_ref, acc_ref)

    @pl.when(pl.program_id(1) == pl.num_programs(1) - 1)
    def _():
        _store_res_ln(alpha * r_ref[...] + acc_ref[...], g_ref, b_ref, xo_ref, xb_ref)


def _ffn_moe_kernel(te_ref, nv_ref, x_ref, w1_ref, w3_ref, w2_ref, o_ref, acc_ref):
    _ffn_accumulate(pl.program_id(0), nv_ref, x_ref, w1_ref, w3_ref, w2_ref, acc_ref)

    @pl.when(pl.program_id(1) == pl.num_programs(1) - 1)
    def _():
        o_ref[...] = acc_ref[...].astype(o_ref.dtype)


def _ffn_specs(tm, d, f, tf):
    nf = f // tf

    def chunk(i, c, te, nv):
        return jnp.where(i < nv[0], c, nf - 1)

    def row(i, c, te, nv):
        return (jnp.minimum(i, nv[0] - 1), 0)

    return [pl.BlockSpec((tm, d), row),
            pl.BlockSpec((1, d, tf), lambda i, c, te, nv: (te[i], 0, chunk(i, c, te, nv))),
            pl.BlockSpec((1, d, tf), lambda i, c, te, nv: (te[i], 0, chunk(i, c, te, nv))),
            pl.BlockSpec((1, tf, d), lambda i, c, te, nv: (te[i], chunk(i, c, te, nv), 0))], nf


def _ffn_dense(xb, x, w1, w3, w2, j, ln_g, ln_b, alpha):
    t, d = x.shape
    f = w1.shape[2]
    tm, tf = _tile(t, 1024), _tile(f, FFN_CHUNK)
    specs, nf = _ffn_specs(tm, d, f, tf)
    row = lambda i, c, te, nv: (i, 0)
    fixed = lambda i, c, te, nv: (0, 0)
    te = jnp.full((t // tm,), j, jnp.int32)
    nv = jnp.full((1,), t // tm, jnp.int32)
    return pl.pallas_call(
        functools.partial(_ffn_dense_kernel, alpha=alpha),
        grid_spec=pltpu.PrefetchScalarGridSpec(
            num_scalar_prefetch=2, grid=(t // tm, nf),
            in_specs=specs + [pl.BlockSpec((tm, d), row), pl.BlockSpec((1, d), fixed), pl.BlockSpec((1, d), fixed)],
            out_specs=[pl.BlockSpec((tm, d), row), pl.BlockSpec((tm, d), row)],
            scratch_shapes=[pltpu.VMEM((tm, d), F32)]),
        out_shape=[jax.ShapeDtypeStruct((t, d), F32), jax.ShapeDtypeStruct((t, d), BF16)],
        compiler_params=_params("parallel", "arbitrary"),
        name="ffn_dense",
    )(te, nv, xb, w1, w3, w2, x, ln_g, ln_b)


def _ffn_moe(xs, te, nv, w1, w3, w2, tm):
    p, d = xs.shape
    f = w1.shape[2]
    tf = _tile(f, FFN_CHUNK)
    specs, nf = _ffn_specs(tm, d, f, tf)
    row = lambda i, c, te, nv: (i, 0)
    return pl.pallas_call(
        _ffn_moe_kernel,
        grid_spec=pltpu.PrefetchScalarGridSpec(
            num_scalar_prefetch=2, grid=(p // tm, nf),
            in_specs=specs,
            out_specs=pl.BlockSpec((tm, d), row),
            scratch_shapes=[pltpu.VMEM((tm, d), F32)]),
        out_shape=jax.ShapeDtypeStruct((p, d), BF16),
        compiler_params=_params("parallel", "arbitrary"),
        name="ffn_moe",
    )(te, nv, xs, w1, w3, w2)


def _mm_res_ln_kernel(a_ref, w_ref, r_ref, g_ref, b_ref, xo_ref, xb_ref, *, alpha):
    h = jnp.dot(a_ref[...], w_ref[...], preferred_element_type=F32)
    _store_res_ln(alpha * r_ref[...] + h, g_ref, b_ref, xo_ref, xb_ref)


def _mm_res_ln(a, w, x, ln_g, ln_b, alpha):
    t, d = x.shape
    k = a.shape[1]
    tm = _tile(t, 512)
    row = lambda i: (i, 0)
    fixed = lambda i: (0, 0)
    return pl.pallas_call(
        functools.partial(_mm_res_ln_kernel, alpha=alpha),
        grid=(t // tm,),
        in_specs=[pl.BlockSpec((tm, k), row), pl.BlockSpec((k, d), fixed), pl.BlockSpec((tm, d), row),
                  pl.BlockSpec((1, d), fixed), pl.BlockSpec((1, d), fixed)],
        out_specs=[pl.BlockSpec((tm, d), row), pl.BlockSpec((tm, d), row)],
        out_shape=[jax.ShapeDtypeStruct((t, d), F32), jax.ShapeDtypeStruct((t, d), BF16)],
        compiler_params=_params("parallel"),
        name="mm_res_ln",
    )(a, w, x, ln_g, ln_b)


def _add_res_ln_kernel(r_ref, o0_ref, o1_ref, gate_ref, g_ref, b_ref, xo_ref, xb_ref, *, alpha):
    h = gate_ref[:, 0:1] * o0_ref[...].astype(F32) + gate_ref[:, 1:2] * o1_ref[...].astype(F32)
    _store_res_ln(alpha * r_ref[...] + h, g_ref, b_ref, xo_ref, xb_ref)


def _add_res_ln(x, o0, o1, gate, ln_g, ln_b, alpha):
    t, d = x.shape
    tm = _tile(t, 1024)
    row = lambda i: (i, 0)
    fixed = lambda i: (0, 0)
    return pl.pallas_call(
        functools.partial(_add_res_ln_kernel, alpha=alpha),
        grid=(t // tm,),
        in_specs=[pl.BlockSpec((tm, d), row), pl.BlockSpec((tm, d), row), pl.BlockSpec((tm, d), row),
                  pl.BlockSpec((tm, LANES), row), pl.BlockSpec((1, d), fixed), pl.BlockSpec((1, d), fixed)],
        out_specs=[pl.BlockSpec((tm, d), row), pl.BlockSpec((tm, d), row)],
        out_shape=[jax.ShapeDtypeStruct((t, d), F32), jax.ShapeDtypeStruct((t, d), BF16)],
        compiler_params=_params("parallel"),
        name="add_res_ln",
    )(x, o0, o1, gate, ln_g, ln_b)


def _qkv_kernel(x_ref, w_ref, qg_ref, kg_ref, cos_ref, sa_ref, sb_ref, q_ref, kt_ref, v_ref, *, n_q):
    tm = x_ref.shape[0]
    acc = jnp.dot(x_ref[...], w_ref[...], preferred_element_type=F32)
    lane = lax.broadcasted_iota(jnp.int32, (tm, LANES), 1)
    left = lane < HEAD_DIM
    cos, sin_a, sin_b = cos_ref[...], sa_ref[...], sb_ref[...]

    def norm_rope(c, g, scale):
        s = c * c
        ss_l = jnp.sum(jnp.where(left, s, 0.0), axis=-1, keepdims=True)
        ss_r = jnp.sum(jnp.where(left, 0.0, s), axis=-1, keepdims=True)
        r = lax.rsqrt(jnp.where(left, ss_l, ss_r) * (1.0 / HEAD_DIM) + RMS_EPS)
        cn = c * r * g
        y = cn * cos + pltpu.roll(cn, LANES - HEAD_DIM // 4, 1) * sin_a + pltpu.roll(cn, HEAD_DIM // 4, 1) * sin_b
        return y * scale

    def split_heads(y, fill):
        return jnp.where(left, y, fill), jnp.where(left, pltpu.roll(y, HEAD_DIM, 1), fill)

    zero = jnp.zeros((tm, LANES), F32)
    for j in range(n_q // 2):
        y = norm_rope(acc[:, j * LANES:(j + 1) * LANES], qg_ref[...], HEAD_DIM ** -0.5)
        even, odd = split_heads(y, zero)
        q_ref[:, (2 * j) * LANES:(2 * j + 1) * LANES] = even.astype(BF16)
        q_ref[:, (2 * j + 1) * LANES:(2 * j + 2) * LANES] = odd.astype(BF16)
    k_off = n_q * HEAD_DIM
    v_off = k_off + N_KV_HEADS * HEAD_DIM
    ones_col = jnp.where(lane == HEAD_DIM, 1.0, 0.0)
    for j in range(N_KV_HEADS // 2):
        y = norm_rope(acc[:, k_off + j * LANES:k_off + (j + 1) * LANES], kg_ref[...], 1.0)
        even, odd = split_heads(y, zero)
        kt_ref[(2 * j) * LANES:(2 * j + 1) * LANES, :] = even.T.astype(BF16)
        kt_ref[(2 * j + 1) * LANES:(2 * j + 2) * LANES, :] = odd.T.astype(BF16)
        even, odd = split_heads(acc[:, v_off + j * LANES:v_off + (j + 1) * LANES], ones_col)
        v_ref[:, (2 * j) * LANES:(2 * j + 1) * LANES] = even.astype(BF16)
        v_ref[:, (2 * j + 1) * LANES:(2 * j + 2) * LANES] = odd.astype(BF16)


def _rope_tables(n):
    pos = jnp.arange(n)
    row = (pos // GRID_W).astype(F32)
    col = (pos % GRID_W).astype(F32)
    sec = HEAD_DIM // 2
    inv = ROPE_THETA ** (-jnp.arange(0, sec, 2, dtype=F32) / sec)
    ang_r = row[:, None] * inv[None, :]
    ang_c = col[:, None] * inv[None, :]
    ang = jnp.concatenate([ang_r, ang_r, ang_c, ang_c], axis=-1)
    ang = jnp.concatenate([ang, ang], axis=-1)
    first = (jnp.arange(LANES) % (HEAD_DIM // 2)) < HEAD_DIM // 4
    sin = jnp.sin(ang)
    return jnp.cos(ang), jnp.where(first, -sin, 0.0), jnp.where(first, 0.0, sin)


def _qkv(xb, w_qkv, q_g2, k_g2, n):
    t, d = xb.shape
    n_q = d // HEAD_DIM
    tm = _tile(n, 512)
    cos, sin_a, sin_b = _rope_tables(n)
    tab = lambda i: (lax.rem(i, n // tm), 0)
    row = lambda i: (i, 0)
    fixed = lambda i: (0, 0)
    return pl.pallas_call(
        functools.partial(_qkv_kernel, n_q=n_q),
        grid=(t // tm,),
        in_specs=[pl.BlockSpec((tm, d), row), pl.BlockSpec(w_qkv.shape, fixed),
                  pl.BlockSpec((1, LANES), fixed), pl.BlockSpec((1, LANES), fixed),
                  pl.BlockSpec((tm, LANES), tab), pl.BlockSpec((tm, LANES), tab), pl.BlockSpec((tm, LANES), tab)],
        out_specs=[pl.BlockSpec((tm, n_q * LANES), row),
                   pl.BlockSpec((N_KV_HEADS * LANES, tm), lambda i: (0, i)),
                   pl.BlockSpec((tm, N_KV_HEADS * LANES), row)],
        out_shape=[jax.ShapeDtypeStruct((t, n_q * LANES), BF16),
                   jax.ShapeDtypeStruct((N_KV_HEADS * LANES, t), BF16),
                   jax.ShapeDtypeStruct((t, N_KV_HEADS * LANES), BF16)],
        compiler_params=_params("parallel"),
        name="qkv_rope",
    )(xb, w_qkv, q_g2, k_g2, cos, sin_a, sin_b)


def _attn_kernel(q_ref, kt_ref, v_ref, o_ref, *, q_per_kv, sub):
    kt = kt_ref[...]
    va = v_ref[...]
    left = lax.broadcasted_iota(jnp.int32, (sub, LANES), 1) < HEAD_DIM
    for r0 in range(0, q_ref.shape[0], sub):
        outs = []
        for h in range(q_per_kv):
            s = jnp.dot(q_ref[r0:r0 + sub, h * LANES:(h + 1) * LANES], kt, preferred_element_type=F32)
            m = jnp.max(s, axis=-1, keepdims=True)
            p = jnp.exp(s - m).astype(BF16)
            o = jnp.dot(p, va, preferred_element_type=F32)
            outs.append(o * (1.0 / o[:, HEAD_DIM:HEAD_DIM + 1]))
        for j in range(q_per_kv // 2):
            pair = jnp.where(left, outs[2 * j], pltpu.roll(outs[2 * j + 1], HEAD_DIM, 1))
            o_ref[r0:r0 + sub, j * LANES:(j + 1) * LANES] = pair.astype(BF16)


def _attention(q, kt, va, n, d):
    t = q.shape[0]
    q_per_kv = d // HEAD_DIM // N_KV_HEADS
    tq = _tile(n, ATTN_Q_TILE)
    nq = n // tq
    return pl.pallas_call(
        functools.partial(_attn_kernel, q_per_kv=q_per_kv, sub=_tile(tq, ATTN_Q_SUB)),
        grid=(t // n, N_KV_HEADS, nq),
        in_specs=[pl.BlockSpec((tq, q_per_kv * LANES), lambda b, h, i: (b * nq + i, h)),
                  pl.BlockSpec((LANES, n), lambda b, h, i: (h, b)),
                  pl.BlockSpec((n, LANES), lambda b, h, i: (b, h))],
        out_specs=pl.BlockSpec((tq, q_per_kv * HEAD_DIM), lambda b, h, i: (b * nq + i, h)),
        out_shape=jax.ShapeDtypeStruct((t, d), BF16),
        compiler_params=_params("parallel", "parallel", "arbitrary"),
        name="attention",
    )(q, kt, va)


def _pool_kernel(x_ref, prev_ref, next_ref, w_ref, cb_ref, cs_ref, g_ref, b_ref, xo_ref, xb_ref,
                 ext_ref, h_ref, *, alpha, n):
    tm, d = x_ref.shape
    halo = prev_ref.shape[0]
    gdim = d // len(POOL_WINDOWS)
    pos0 = lax.rem(pl.program_id(0) * tm, n)
    x = x_ref[...]
    ext_ref[0:halo, :] = jnp.where(pos0 > 0, prev_ref[...], 0.0)
    ext_ref[halo:halo + tm, :] = x
    ext_ref[halo + tm:halo + tm + halo, :] = jnp.where(pos0 + tm < n, next_ref[...], 0.0)
    pos = pos0 + lax.broadcasted_iota(jnp.int32, (tm, 1), 0)
    for gi, w in enumerate(POOL_WINDOWS):
        cols = slice(gi * gdim, (gi + 1) * gdim)
        wsum = ext_ref[halo - w // 2:halo - w // 2 + tm, cols]
        for j in range(1 - w // 2, w - w // 2):
            wsum = wsum + ext_ref[halo + j:halo + j + tm, cols]
        cnt = (jnp.minimum(pos + (w - w // 2), n) - jnp.maximum(pos - w // 2, 0)).astype(F32)
        mix = (wsum * (1.0 / cnt) - x[:, cols]).astype(BF16)
        y = jnp.dot(mix, w_ref[gi], preferred_element_type=F32) + cb_ref[:, cols]
        h_ref[:, cols] = y * cs_ref[:, cols]
    _store_res_ln(alpha * x + h_ref[...], g_ref, b_ref, xo_ref, xb_ref)


def _pool_mixer(x, c_w, c_b, c_scale, ln_g, ln_b, alpha, n):
    t, d = x.shape
    halo = SUBLANES_F32
    assert max(POOL_WINDOWS) // 2 <= halo
    tm = _tile(n, 512)
    hb = tm // halo
    last = t // halo - 1
    row = lambda i: (i, 0)
    fixed = lambda i: (0, 0)
    gdim = d // len(POOL_WINDOWS)
    return pl.pallas_call(
        functools.partial(_pool_kernel, alpha=alpha, n=n),
        grid=(t // tm,),
        in_specs=[pl.BlockSpec((tm, d), row),
                  pl.BlockSpec((halo, d), lambda i: (jnp.maximum(i * hb - 1, 0), 0)),
                  pl.BlockSpec((halo, d), lambda i: (jnp.minimum((i + 1) * hb, last), 0)),
                  pl.BlockSpec((len(POOL_WINDOWS), gdim, gdim), lambda i: (0, 0, 0)),
                  pl.BlockSpec((1, d), fixed), pl.BlockSpec((1, d), fixed),
                  pl.BlockSpec((1, d), fixed), pl.BlockSpec((1, d), fixed)],
        out_specs=[pl.BlockSpec((tm, d), row), pl.BlockSpec((tm, d), row)],
        out_shape=[jax.ShapeDtypeStruct((t, d), F32), jax.ShapeDtypeStruct((t, d), BF16)],
        scratch_shapes=[pltpu.VMEM((tm + 2 * halo, d), F32), pltpu.VMEM((tm, d), F32)],
        compiler_params=_params("parallel"),
        name="pool_mixer",
    )(x, x, x, c_w, c_b, c_scale, ln_g, ln_b)


def _conv_kernel(h_ref, prev_ref, next_ref, x_ref, wdw_ref, bdw_ref, cg_ref, cb_ref, wo_ref, bo_ref, g_ref, b_ref,
                 xo_ref, xb_ref, ext_ref, conv_ref, *, alpha, n, rows, cols):
    tm, d = h_ref.shape
    halo = prev_ref.shape[0]
    width = wdw_ref.shape[0] // SUBLANES_F32
    pos0 = lax.rem(pl.program_id(0) * tm, n)
    ext_ref[0, 0:halo, :] = jnp.where(pos0 > 0, prev_ref[...].astype(F32), 0.0)
    ext_ref[0, halo:halo + tm, :] = h_ref[...].astype(F32)
    ext_ref[0, halo + tm:halo + tm + halo, :] = jnp.where(pos0 + tm < n, next_ref[...].astype(F32), 0.0)
    span = tm + 2 * halo - SUBLANES_F32
    for s in range(1, SUBLANES_F32):
        ext_ref[s, 0:span, :] = ext_ref[0, s:s + span, :]
    first = halo - width // 2

    sub = SUBLANES_F32

    def chunk(r, carry):
        r0 = pl.multiple_of(r * rows, rows)
        for c0 in range(0, d, cols):
            acc = [None] * (rows // sub)
            for k in range(width):
                s = (first + k) % sub
                w = wdw_ref[k * sub:(k + 1) * sub, c0:c0 + cols]
                for g in range(rows // sub):
                    tap = ext_ref[s, pl.ds(r0 + (first + k - s) + g * sub, sub), c0:c0 + cols] * w
                    acc[g] = tap if acc[g] is None else acc[g] + tap
            for g in range(rows // sub):
                conv_ref[pl.ds(r0 + g * sub, sub), c0:c0 + cols] = acc[g]
        return carry

    lax.fori_loop(0, tm // rows, chunk, 0)
    c = _silu(_ln_rows(conv_ref[...] + bdw_ref[...], cg_ref[...], cb_ref[...])).astype(BF16)
    h = jnp.dot(c, wo_ref[...], preferred_element_type=F32) + bo_ref[...]
    _store_res_ln(alpha * x_ref[...] + h, g_ref, b_ref, xo_ref, xb_ref)


def _conv_mixer(hg, x, w_dw, b_dw, c_g, c_b, w_out, b_out, ln_g, ln_b, alpha, n):
    t, d = x.shape
    halo = SUBLANES_BF16
    assert w_dw.shape[0] // SUBLANES_F32 // 2 <= halo
    tm = _tile(n, 512)
    hb = tm // halo
    last = t // halo - 1
    row = lambda i: (i, 0)
    fixed = lambda i: (0, 0)
    vec = pl.BlockSpec((1, d), fixed)
    return pl.pallas_call(
        functools.partial(_conv_kernel, alpha=alpha, n=n, rows=4 * SUBLANES_F32, cols=4 * LANES),
        grid=(t // tm,),
        in_specs=[pl.BlockSpec((tm, d), row),
                  pl.BlockSpec((halo, d), lambda i: (jnp.maximum(i * hb - 1, 0), 0)),
                  pl.BlockSpec((halo, d), lambda i: (jnp.minimum((i + 1) * hb, last), 0)),
                  pl.BlockSpec((tm, d), row),
                  pl.BlockSpec(w_dw.shape, fixed), vec, vec, vec,
                  pl.BlockSpec((d, d), fixed), vec, vec, vec],
        out_specs=[pl.BlockSpec((tm, d), row), pl.BlockSpec((tm, d), row)],
        out_shape=[jax.ShapeDtypeStruct((t, d), F32), jax.ShapeDtypeStruct((t, d), BF16)],
        scratch_shapes=[pltpu.VMEM((SUBLANES_F32, tm + 2 * halo, d), F32), pltpu.VMEM((tm, d), F32)],
        compiler_params=_params("parallel"),
        name="conv_mixer",
    )(hg, hg, hg, x, w_dw, b_dw, c_g, c_b, w_out, b_out, ln_g, ln_b)


def _router_kernel(x_ref, w_ref, b_ref, tri_ref, idx_ref, gate_ref, cnt_ref, run_ref, *, n_experts):
    @pl.when(pl.program_id(0) == 0)
    def _():
        run_ref[...] = jnp.zeros_like(run_ref)

    x = x_ref[...]
    x_hi = x.astype(BF16)
    x_lo = (x - x_hi.astype(F32)).astype(BF16)
    hi = jnp.dot(x_hi, w_ref[...], preferred_element_type=F32)
    lo = jnp.dot(x_lo, w_ref[...], preferred_element_type=F32)
    logits = hi[:, :LANES] + (hi[:, LANES:] + lo[:, :LANES]) + b_ref[...]
    lane = lax.broadcasted_iota(jnp.int32, logits.shape, 1)
    neg = jnp.float32(-jnp.inf)
    logits = jnp.where(lane < n_experts, logits, neg)
    m1 = jnp.max(logits, axis=-1, keepdims=True)
    i1 = jnp.min(jnp.where(logits == m1, lane, LANES), axis=-1, keepdims=True)
    rest = jnp.where(lane == i1, neg, logits)
    m2 = jnp.max(rest, axis=-1, keepdims=True)
    i2 = jnp.min(jnp.where(rest == m2, lane, LANES), axis=-1, keepdims=True)
    e2 = jnp.exp(m2 - m1)
    den = 1.0 + e2
    gate_ref[...] = jnp.where(lane == 0, 1.0 / den, jnp.where(lane == 1, e2 / den, 0.0))
    chosen = jnp.where((lane == i1) | (lane == i2), 1.0, 0.0)
    before = jnp.dot(tri_ref[...], chosen.astype(BF16), preferred_element_type=F32) + run_ref[...]
    r1 = jnp.sum(jnp.where(lane == i1, before, 0.0), axis=-1, keepdims=True).astype(jnp.int32)
    r2 = jnp.sum(jnp.where(lane == i2, before, 0.0), axis=-1, keepdims=True).astype(jnp.int32)
    idx_ref[...] = jnp.where(lane == 0, i1, jnp.where(lane == 1, i2, jnp.where(lane == 2, r1, jnp.where(lane == 3, r2, 0))))
    run = run_ref[...] + jnp.sum(chosen, axis=0, keepdims=True)
    run_ref[...] = run
    cnt_ref[...] = run.astype(jnp.int32)


def _router(x, w_r, b_r):
    t, d = x.shape
    n_experts = w_r.shape[1]
    w_pad = jnp.zeros((d, LANES), F32).at[:, :n_experts].set(w_r)
    w_hi = w_pad.astype(BF16)
    w_split = jnp.concatenate([w_hi, (w_pad - w_hi.astype(F32)).astype(BF16)], axis=1)
    b_pad = jnp.zeros((1, LANES), F32).at[0, :n_experts].set(b_r)
    tm = _tile(t, 1024)
    tri = (jnp.arange(tm)[:, None] > jnp.arange(tm)[None, :]).astype(BF16)
    row = lambda i: (i, 0)
    fixed = lambda i: (0, 0)
    return pl.pallas_call(
        functools.partial(_router_kernel, n_experts=n_experts),
        grid=(t // tm,),
        in_specs=[pl.BlockSpec((tm, d), row), pl.BlockSpec((d, 2 * LANES), fixed), pl.BlockSpec((1, LANES), fixed),
                  pl.BlockSpec((tm, tm), fixed)],
        out_specs=[pl.BlockSpec((tm, LANES), row), pl.BlockSpec((tm, LANES), row), pl.BlockSpec((1, LANES), fixed)],
        out_shape=[jax.ShapeDtypeStruct((t, LANES), jnp.int32), jax.ShapeDtypeStruct((t, LANES), F32),
                   jax.ShapeDtypeStruct((1, LANES), jnp.int32)],
        scratch_shapes=[pltpu.VMEM((1, LANES), F32)],
        compiler_params=_params("arbitrary"),
        name="router",
    )(x, w_split, b_pad, tri)


def _moe_dispatch(idx, counts, n_experts, tm):
    t = idx.shape[0]
    s = t * TOP_K
    padded = ((counts + tm - 1) // tm) * tm
    pend = jnp.cumsum(padded)
    pstart = pend - padded
    onehot = idx[:, :TOP_K, None] == jnp.arange(n_experts, dtype=jnp.int32)
    dest = jnp.sum(jnp.where(onehot, pstart, 0), axis=-1) + idx[:, TOP_K:2 * TOP_K]
    n_tiles = s // tm + n_experts
    tok = jnp.broadcast_to(jnp.arange(t, dtype=jnp.int32)[:, None], (t, TOP_K))
    src_tok = jnp.zeros((n_tiles * tm,), jnp.int32).at[dest.reshape(s)].set(
        tok.reshape(s), unique_indices=True, mode="promise_in_bounds")
    nv = (pend[-1] // tm).astype(jnp.int32)
    tile_idx = jnp.arange(n_tiles, dtype=jnp.int32)
    te = jnp.searchsorted(pend, jnp.minimum(tile_idx, nv - 1) * tm, side="right").astype(jnp.int32)
    return src_tok, te, nv.reshape(1), dest


def _moe_layer(x, xb, w_r, b_r, w1, w3, w2, j, ln_g, ln_b, alpha):
    t, d = x.shape
    n_experts = w_r.shape[1]
    tm = _tile(t * TOP_K, 1024)
    idx, gate, counts = _router(x, w_r, b_r)
    src_tok, te, nv, dest = _moe_dispatch(idx, counts[0, :n_experts], n_experts, tm)
    rows = lambda a, r: a.at[r].get(mode="promise_in_bounds")
    xs = rows(xb, src_tok)
    out_p = _ffn_moe(xs, te + j * n_experts, nv, w1, w3, w2, tm)
    o0 = rows(out_p, dest[:, 0])
    o1 = rows(out_p, dest[:, 1])
    return _add_res_ln(x, o0, o1, gate, ln_g, ln_b, alpha)


def kernel(x_prompt, x_sample, ln_mix_g, ln_mix_b, ln_ffn_g, ln_ffn_b, a_w_in, a_b_in, a_ln_g, a_ln_b, a_w_s, a_b_s, a_w_out, b_w_qkv, b_q_g, b_k_g, b_w_o, c_w, c_b, c_scale, d_w_in, d_b_in, d_w_dw, d_b_dw, d_ln_g, d_ln_b, d_w_out, d_b_out, f_w1, f_w3, f_w2, m_w_r, m_b_r, m_w1, m_w3, m_w2):
    d = x_prompt.shape[-1]
    depth = ln_mix_g.shape[0]
    alpha = (2.0 * depth) ** 0.25
    vec = lambda v: v.reshape(1, -1).astype(F32)
    two_heads = lambda g: jnp.tile(g.astype(F32), 2).reshape(1, LANES)

    a_w_in, a_w_out, b_w_qkv, b_w_o, d_w_in, d_w_out = map(_to_bf16, (a_w_in, a_w_out, b_w_qkv, b_w_o, d_w_in, d_w_out))
    f_w1, f_w3, f_w2 = map(_to_bf16, (f_w1, f_w3, f_w2))
    n_moe, n_experts = m_w1.shape[:2]
    m_w1, m_w3, m_w2 = (_to_bf16(w).reshape((n_moe * n_experts,) + w.shape[2:]) for w in (m_w1, m_w3, m_w2))
    a_w_s, c_w = a_w_s.astype(BF16), c_w.astype(BF16)
    d_w_dw = jnp.repeat(d_w_dw.astype(F32), SUBLANES_F32, axis=1)

    groups = []
    for xin in (x_prompt, x_sample):
        bsz, n, _ = xin.shape
        x = xin.reshape(bsz * n, d)
        groups.append([x, _to_bf16(x), n])
    for l in range(depth):
        kind, i = l % 4, l // 4
        mg, mb = vec(ln_mix_g[l]), vec(ln_mix_b[l])
        fg, fb = vec(ln_ffn_g[l]), vec(ln_ffn_b[l])
        j = l // 2
        for grp in groups:
            x, xb, n = grp
            if kind == 0:
                z = _mm_gelu(xb, a_w_in[i], vec(a_b_in[i]))
                x, xb = _gmlp_gate(z, x, vec(a_ln_g[i]), vec(a_ln_b[i]), a_w_s[i], a_b_s[i].T.astype(F32),
                                   a_w_out[i], mg, mb, alpha)
            elif kind == 1:
                q, kt, va = _qkv(xb, b_w_qkv[i], two_heads(b_q_g[i]), two_heads(b_k_g[i]), n)
                o = _attention(q, kt, va, n, d)
                x, xb = _mm_res_ln(o, b_w_o[i], x, mg, mb, alpha)
            elif kind == 2:
                x, xb = _pool_mixer(x, c_w[i], vec(c_b[i]), vec(c_scale[i]), mg, mb, alpha, n)
            else:
                hg = _mm_glu(xb, d_w_in[i], vec(d_b_in[i]))
                x, xb = _conv_mixer(hg, x, d_w_dw[i], vec(d_b_dw[i]), vec(d_ln_g[i]), vec(d_ln_b[i]),
                                    d_w_out[i], vec(d_b_out[i]), mg, mb, alpha, n)
            if l % 2 == 0:
                x, xb = _ffn_dense(xb, x, f_w1, f_w3, f_w2, j, fg, fb, alpha)
            else:
                x, xb = _moe_layer(x, xb, m_w_r[j], m_b_r[j], m_w1, m_w3, m_w2, j, fg, fb, alpha)
            grp[0], grp[1] = x, xb
    return tuple(x.reshape(xin.shape) for (x, _, _), xin in zip(groups, (x_prompt, x_sample)))
```

```python
import functools
import math

import jax
import jax.numpy as jnp
from jax import lax
from jax.experimental import pallas as pl
from jax.experimental.pallas import tpu as pltpu

F32 = jnp.float32
BF16 = jnp.bfloat16

GRID_W = 64
BLOCK = 128
A_GROUPS = 8
HEAD_DIM = 64
N_KV_HEADS = 4
ROPE_THETA = 10000.0
POOL_WINDOWS = (2, 4, 8, 16)
TOP_K = 2
LN_EPS = 1e-5
RMS_EPS = 1e-6

LANES = 128
SUBLANES_F32 = 8
SUBLANES_BF16 = 16
VMEM_LIMIT = 56 * 1024 * 1024
FFN_CHUNK = 512
ATTN_Q_TILE = 512
ATTN_Q_SUB = 256


def _params(*sem):
    return pltpu.CompilerParams(dimension_semantics=sem, vmem_limit_bytes=VMEM_LIMIT)


def _tile(total, target):
    t = math.gcd(total, target)
    assert t % SUBLANES_BF16 == 0, (total, target)
    return t


def _ln_rows(y, g, b):
    mu = jnp.mean(y, axis=-1, keepdims=True)
    yc = y - mu
    var = jnp.mean(yc * yc, axis=-1, keepdims=True)
    return yc * lax.rsqrt(var + LN_EPS) * g + b


def _gelu_tanh(x):
    c = math.sqrt(2.0 / math.pi)
    return x * (0.5 * (1.0 + jnp.tanh(c * (x + 0.044715 * (x * x * x)))))


def _silu(x):
    return x * jax.nn.sigmoid(x)


def _cast_kernel(x_ref, o_ref):
    o_ref[...] = x_ref[...].astype(o_ref.dtype)


def _to_bf16(w):
    cols = w.shape[-1]
    rows = w.size // cols
    tm = _tile(rows, 512)
    out = pl.pallas_call(
        _cast_kernel,
        grid=(rows // tm,),
        in_specs=[pl.BlockSpec((tm, cols), lambda i: (i, 0))],
        out_specs=pl.BlockSpec((tm, cols), lambda i: (i, 0)),
        out_shape=jax.ShapeDtypeStruct((rows, cols), BF16),
        compiler_params=_params("parallel"),
        name="to_bf16",
    )(w.reshape(rows, cols))
    return out.reshape(w.shape)


def _store_res_ln(y, g_ref, b_ref, xo_ref, xb_ref):
    out = _ln_rows(y, g_ref[...], b_ref[...])
    xo_ref[...] = out
    xb_ref[...] = out.astype(BF16)


def _mm_gelu_kernel(x_ref, w_ref, b_ref, o_ref, *, sub):
    x = x_ref[...]
    for c0 in range(0, o_ref.shape[1], sub):
        z = jnp.dot(x, w_ref[:, c0:c0 + sub], preferred_element_type=F32) + b_ref[:, c0:c0 + sub]
        o_ref[:, c0:c0 + sub] = _gelu_tanh(z).astype(o_ref.dtype)


def _mm_gelu(xb, w, b):
    t, k = xb.shape
    n = w.shape[1]
    tm, tn = _tile(t, 1024), _tile(n, 2048)
    return pl.pallas_call(
        functools.partial(_mm_gelu_kernel, sub=_tile(tn, 1024)),
        grid=(n // tn, t // tm),
        in_specs=[pl.BlockSpec((tm, k), lambda j, i: (i, 0)),
                  pl.BlockSpec((k, tn), lambda j, i: (0, j)),
                  pl.BlockSpec((1, tn), lambda j, i: (0, j))],
        out_specs=pl.BlockSpec((tm, tn), lambda j, i: (i, j)),
        out_shape=jax.ShapeDtypeStruct((t, n), BF16),
        compiler_params=_params("parallel", "parallel"),
        name="mm_gelu",
    )(xb, w, b)


def _mm_glu_kernel(x_ref, wa_ref, wg_ref, ba_ref, bg_ref, o_ref):
    x = x_ref[...]
    a = jnp.dot(x, wa_ref[...], preferred_element_type=F32) + ba_ref[...]
    g = jnp.dot(x, wg_ref[...], preferred_element_type=F32) + bg_ref[...]
    o_ref[...] = (a * jax.nn.sigmoid(g)).astype(o_ref.dtype)


def _mm_glu(xb, w, b):
    t, k = xb.shape
    n = w.shape[1] // 2
    tm, tn = _tile(t, 1024), _tile(n, 512)
    nj = n // tn
    return pl.pallas_call(
        _mm_glu_kernel,
        grid=(nj, t // tm),
        in_specs=[pl.BlockSpec((tm, k), lambda j, i: (i, 0)),
                  pl.BlockSpec((k, tn), lambda j, i: (0, j)),
                  pl.BlockSpec((k, tn), lambda j, i: (0, j + nj)),
                  pl.BlockSpec((1, tn), lambda j, i: (0, j)),
                  pl.BlockSpec((1, tn), lambda j, i: (0, j + nj))],
        out_specs=pl.BlockSpec((tm, tn), lambda j, i: (i, j)),
        out_shape=jax.ShapeDtypeStruct((t, n), BF16),
        compiler_params=_params("parallel", "parallel"),
        name="mm_glu",
    )(xb, w, w, b, b)


def _gmlp_gate_kernel(z_ref, x_ref, vg_ref, vb_ref, ws_ref, bs_ref, wo_ref, g_ref, b_ref,
                      xo_ref, xb_ref, gated_ref, *, alpha, half, gdim):
    tm = z_ref.shape[0]
    v = _ln_rows(z_ref[:, half:].astype(F32), vg_ref[...], vb_ref[...]).astype(BF16)
    for c in range(tm // BLOCK):
        rows = slice(c * BLOCK, (c + 1) * BLOCK)
        for gi in range(A_GROUPS):
            cols = slice(gi * gdim, (gi + 1) * gdim)
            sv = jnp.dot(ws_ref[gi], v[rows, cols], preferred_element_type=F32) + bs_ref[:, gi:gi + 1]
            gated_ref[rows, cols] = (z_ref[rows, cols].astype(F32) * sv).astype(BF16)
    h = jnp.dot(gated_ref[...], wo_ref[...], preferred_element_type=F32)
    _store_res_ln(alpha * x_ref[...] + h, g_ref, b_ref, xo_ref, xb_ref)


def _gmlp_gate(z, x, v_g, v_b, w_s, b_s_t, w_out, ln_g, ln_b, alpha):
    t, d = x.shape
    half = z.shape[1] // 2
    tm = _tile(t, 2 * BLOCK)
    row = lambda i: (i, 0)
    fixed2 = lambda i: (0, 0)
    return pl.pallas_call(
        functools.partial(_gmlp_gate_kernel, alpha=alpha, half=half, gdim=half // A_GROUPS),
        grid=(t // tm,),
        in_specs=[pl.BlockSpec((tm, 2 * half), row),
                  pl.BlockSpec((tm, d), row),
                  pl.BlockSpec((1, half), fixed2),
                  pl.BlockSpec((1, half), fixed2),
                  pl.BlockSpec((A_GROUPS, BLOCK, BLOCK), lambda i: (0, 0, 0)),
                  pl.BlockSpec((BLOCK, A_GROUPS), fixed2),
                  pl.BlockSpec((half, d), fixed2),
                  pl.BlockSpec((1, d), fixed2),
                  pl.BlockSpec((1, d), fixed2)],
        out_specs=[pl.BlockSpec((tm, d), row), pl.BlockSpec((tm, d), row)],
        out_shape=[jax.ShapeDtypeStruct((t, d), F32), jax.ShapeDtypeStruct((t, d), BF16)],
        scratch_shapes=[pltpu.VMEM((tm, half), BF16)],
        compiler_params=_params("parallel"),
        name="gmlp_gate",
    )(z, x, v_g, v_b, w_s, b_s_t, w_out, ln_g, ln_b)


def _swiglu(x, w1_ref, w3_ref, w2_ref, tf):
    acc = None
    for c0 in range(0, w1_ref.shape[2], tf):
        a = jnp.dot(x, w1_ref[0, :, c0:c0 + tf], preferred_element_type=F32)
        b = jnp.dot(x, w3_ref[0, :, c0:c0 + tf], preferred_element_type=F32)
        mid = (_silu(a) * b).astype(BF16)
        part = jnp.dot(mid, w2_ref[0, c0:c0 + tf, :], preferred_element_type=F32)
        acc = part if acc is None else acc + part
    return acc


def _ffn_dense_kernel(x_ref, w1_ref, w3_ref, w2_ref, r_ref, g_ref, b_ref, xo_ref, xb_ref, *, alpha, tf):
    h = _swiglu(x_ref[...], w1_ref, w3_ref, w2_ref, tf)
    _store_res_ln(alpha * r_ref[...] + h, g_ref, b_ref, xo_ref, xb_ref)


def _ffn_moe_kernel(te_ref, nv_ref, x_ref, w1_ref, w3_ref, w2_ref, o_ref, *, tf):
    valid = pl.program_id(0) < nv_ref[0]

    @pl.when(valid)
    def _():
        o_ref[...] = _swiglu(x_ref[...], w1_ref, w3_ref, w2_ref, tf).astype(o_ref.dtype)

    @pl.when(jnp.logical_not(valid))
    def _():
        o_ref[...] = jnp.zeros_like(o_ref)


def _ffn_dense(xb, x, w1, w3, w2, j, ln_g, ln_b, alpha):
    t, d = x.shape
    f = w1.shape[2]
    tm, tf = _tile(t, 512), _tile(f, FFN_CHUNK)
    row = lambda i: (i, 0)
    fixed = lambda i: (0, 0)
    layer = lambda i: (j, 0, 0)
    once = pl.Buffered(1)
    return pl.pallas_call(
        functools.partial(_ffn_dense_kernel, alpha=alpha, tf=tf),
        grid=(t // tm,),
        in_specs=[pl.BlockSpec((tm, d), row),
                  pl.BlockSpec((1, d, f), layer, pipeline_mode=once),
                  pl.BlockSpec((1, d, f), layer, pipeline_mode=once),
                  pl.BlockSpec((1, f, d), layer, pipeline_mode=once),
                  pl.BlockSpec((tm, d), row), pl.BlockSpec((1, d), fixed), pl.BlockSpec((1, d), fixed)],
        out_specs=[pl.BlockSpec((tm, d), row), pl.BlockSpec((tm, d), row)],
        out_shape=[jax.ShapeDtypeStruct((t, d), F32), jax.ShapeDtypeStruct((t, d), BF16)],
        compiler_params=_params("parallel"),
        name="ffn_dense",
    )(xb, w1, w3, w2, x, ln_g, ln_b)


def _ffn_moe(xs, te, nv, w1, w3, w2, tm):
    p, d = xs.shape
    f = w1.shape[2]
    expert = lambda i, te, nv: (te[i], 0, 0)
    once = pl.Buffered(1)
    return pl.pallas_call(
        functools.partial(_ffn_moe_kernel, tf=_tile(f, FFN_CHUNK)),
        grid_spec=pltpu.PrefetchScalarGridSpec(
            num_scalar_prefetch=2, grid=(p // tm,),
            in_specs=[pl.BlockSpec((tm, d), lambda i, te, nv: (jnp.minimum(i, nv[0] - 1), 0)),
                      pl.BlockSpec((1, d, f), expert, pipeline_mode=once),
                      pl.BlockSpec((1, d, f), expert, pipeline_mode=once),
                      pl.BlockSpec((1, f, d), expert, pipeline_mode=once)],
            out_specs=pl.BlockSpec((tm, d), lambda i, te, nv: (i, 0))),
        out_shape=jax.ShapeDtypeStruct((p, d), BF16),
        compiler_params=_params("arbitrary"),
        name="ffn_moe",
    )(te, nv, xs, w1, w3, w2)


def _mm_res_ln_kernel(a_ref, w_ref, r_ref, g_ref, b_ref, xo_ref, xb_ref, *, alpha):
    h = jnp.dot(a_ref[...], w_ref[...], preferred_element_type=F32)
    _store_res_ln(alpha * r_ref[...] + h, g_ref, b_ref, xo_ref, xb_ref)


def _mm_res_ln(a, w, x, ln_g, ln_b, alpha):
    t, d = x.shape
    k = a.shape[1]
    tm = _tile(t, 512)
    row = lambda i: (i, 0)
    fixed = lambda i: (0, 0)
    return pl.pallas_call(
        functools.partial(_mm_res_ln_kernel, alpha=alpha),
        grid=(t // tm,),
        in_specs=[pl.BlockSpec((tm, k), row), pl.BlockSpec((k, d), fixed), pl.BlockSpec((tm, d), row),
                  pl.BlockSpec((1, d), fixed), pl.BlockSpec((1, d), fixed)],
        out_specs=[pl.BlockSpec((tm, d), row), pl.BlockSpec((tm, d), row)],
        out_shape=[jax.ShapeDtypeStruct((t, d), F32), jax.ShapeDtypeStruct((t, d), BF16)],
        compiler_params=_params("parallel"),
        name="mm_res_ln",
    )(a, w, x, ln_g, ln_b)


def _add_res_ln_kernel(r_ref, o0_ref, o1_ref, gate_ref, g_ref, b_ref, xo_ref, xb_ref, *, alpha):
    h = gate_ref[:, 0:1] * o0_ref[...].astype(F32) + gate_ref[:, 1:2] * o1_ref[...].astype(F32)
    _store_res_ln(alpha * r_ref[...] + h, g_ref, b_ref, xo_ref, xb_ref)


def _add_res_ln(x, o0, o1, gate, ln_g, ln_b, alpha):
    t, d = x.shape
    tm = _tile(t, 1024)
    row = lambda i: (i, 0)
    fixed = lambda i: (0, 0)
    return pl.pallas_call(
        functools.partial(_add_res_ln_kernel, alpha=alpha),
        grid=(t // tm,),
        in_specs=[pl.BlockSpec((tm, d), row), pl.BlockSpec((tm, d), row), pl.BlockSpec((tm, d), row),
                  pl.BlockSpec((tm, LANES), row), pl.BlockSpec((1, d), fixed), pl.BlockSpec((1, d), fixed)],
        out_specs=[pl.BlockSpec((tm, d), row), pl.BlockSpec((tm, d), row)],
        out_shape=[jax.ShapeDtypeStruct((t, d), F32), jax.ShapeDtypeStruct((t, d), BF16)],
        compiler_params=_params("parallel"),
        name="add_res_ln",
    )(x, o0, o1, gate, ln_g, ln_b)


def _qkv_kernel(x_ref, w_ref, qg_ref, kg_ref, cos_ref, sa_ref, sb_ref, q_ref, kt_ref, v_ref, *, n_q):
    tm = x_ref.shape[0]
    acc = jnp.dot(x_ref[...], w_ref[...], preferred_element_type=F32)
    lane = lax.broadcasted_iota(jnp.int32, (tm, LANES), 1)
    left = lane < HEAD_DIM
    cos, sin_a, sin_b = cos_ref[...], sa_ref[...], sb_ref[...]

    def norm_rope(c, g, scale):
        s = c * c
        ss_l = jnp.sum(jnp.where(left, s, 0.0), axis=-1, keepdims=True)
        ss_r = jnp.sum(jnp.where(left, 0.0, s), axis=-1, keepdims=True)
        r = lax.rsqrt(jnp.where(left, ss_l, ss_r) * (1.0 / HEAD_DIM) + RMS_EPS)
        cn = c * r * g
        y = cn * cos + pltpu.roll(cn, LANES - HEAD_DIM // 4, 1) * sin_a + pltpu.roll(cn, HEAD_DIM // 4, 1) * sin_b
        return y * scale

    def split_heads(y, fill):
        return jnp.where(left, y, fill), jnp.where(left, pltpu.roll(y, HEAD_DIM, 1), fill)

    zero = jnp.zeros((tm, LANES), F32)
    for j in range(n_q // 2):
        y = norm_rope(acc[:, j * LANES:(j + 1) * LANES], qg_ref[...], HEAD_DIM ** -0.5)
        even, odd = split_heads(y, zero)
        q_ref[:, (2 * j) * LANES:(2 * j + 1) * LANES] = even.astype(BF16)
        q_ref[:, (2 * j + 1) * LANES:(2 * j + 2) * LANES] = odd.astype(BF16)
    k_off = n_q * HEAD_DIM
    v_off = k_off + N_KV_HEADS * HEAD_DIM
    ones_col = jnp.where(lane == HEAD_DIM, 1.0, 0.0)
    for j in range(N_KV_HEADS // 2):
        y = norm_rope(acc[:, k_off + j * LANES:k_off + (j + 1) * LANES], kg_ref[...], 1.0)
        even, odd = split_heads(y, zero)
        kt_ref[(2 * j) * LANES:(2 * j + 1) * LANES, :] = even.T.astype(BF16)
        kt_ref[(2 * j + 1) * LANES:(2 * j + 2) * LANES, :] = odd.T.astype(BF16)
        even, odd = split_heads(acc[:, v_off + j * LANES:v_off + (j + 1) * LANES], ones_col)
        v_ref[:, (2 * j) * LANES:(2 * j + 1) * LANES] = even.astype(BF16)
        v_ref[:, (2 * j + 1) * LANES:(2 * j + 2) * LANES] = odd.astype(BF16)


def _rope_tables(n):
    pos = jnp.arange(n)
    row = (pos // GRID_W).astype(F32)
    col = (pos % GRID_W).astype(F32)
    sec = HEAD_DIM // 2
    inv = ROPE_THETA ** (-jnp.arange(0, sec, 2, dtype=F32) / sec)
    ang_r = row[:, None] * inv[None, :]
    ang_c = col[:, None] * inv[None, :]
    ang = jnp.concatenate([ang_r, ang_r, ang_c, ang_c], axis=-1)
    ang = jnp.concatenate([ang, ang], axis=-1)
    first = (jnp.arange(LANES) % (HEAD_DIM // 2)) < HEAD_DIM // 4
    sin = jnp.sin(ang)
    return jnp.cos(ang), jnp.where(first, -sin, 0.0), jnp.where(first, 0.0, sin)


def _qkv(xb, w_qkv, q_g2, k_g2, n):
    t, d = xb.shape
    n_q = d // HEAD_DIM
    tm = _tile(n, 512)
    cos, sin_a, sin_b = _rope_tables(n)
    tab = lambda i: (lax.rem(i, n // tm), 0)
    row = lambda i: (i, 0)
    fixed = lambda i: (0, 0)
    return pl.pallas_call(
        functools.partial(_qkv_kernel, n_q=n_q),
        grid=(t // tm,),
        in_specs=[pl.BlockSpec((tm, d), row), pl.BlockSpec(w_qkv.shape, fixed),
                  pl.BlockSpec((1, LANES), fixed), pl.BlockSpec((1, LANES), fixed),
                  pl.BlockSpec((tm, LANES), tab), pl.BlockSpec((tm, LANES), tab), pl.BlockSpec((tm, LANES), tab)],
        out_specs=[pl.BlockSpec((tm, n_q * LANES), row),
                   pl.BlockSpec((N_KV_HEADS * LANES, tm), lambda i: (0, i)),
                   pl.BlockSpec((tm, N_KV_HEADS * LANES), row)],
        out_shape=[jax.ShapeDtypeStruct((t, n_q * LANES), BF16),
                   jax.ShapeDtypeStruct((N_KV_HEADS * LANES, t), BF16),
                   jax.ShapeDtypeStruct((t, N_KV_HEADS * LANES), BF16)],
        compiler_params=_params("parallel"),
        name="qkv_rope",
    )(xb, w_qkv, q_g2, k_g2, cos, sin_a, sin_b)


def _attn_kernel(q_ref, kt_ref, v_ref, o_ref, *, q_per_kv, sub):
    kt = kt_ref[...]
    va = v_ref[...]
    left = lax.broadcasted_iota(jnp.int32, (sub, LANES), 1) < HEAD_DIM
    for r0 in range(0, q_ref.shape[0], sub):
        outs = []
        for h in range(q_per_kv):
            s = jnp.dot(q_ref[r0:r0 + sub, h * LANES:(h + 1) * LANES], kt, preferred_element_type=F32)
            m = jnp.max(s, axis=-1, keepdims=True)
            p = jnp.exp(s - m).astype(BF16)
            o = jnp.dot(p, va, preferred_element_type=F32)
            outs.append(o * (1.0 / o[:, HEAD_DIM:HEAD_DIM + 1]))
        for j in range(q_per_kv // 2):
            pair = jnp.where(left, outs[2 * j], pltpu.roll(outs[2 * j + 1], HEAD_DIM, 1))
            o_ref[r0:r0 + sub, j * LANES:(j + 1) * LANES] = pair.astype(BF16)


def _attention(q, kt, va, n, d):
    t = q.shape[0]
    q_per_kv = d // HEAD_DIM // N_KV_HEADS
    tq = _tile(n, ATTN_Q_TILE)
    nq = n // tq
    return pl.pallas_call(
        functools.partial(_attn_kernel, q_per_kv=q_per_kv, sub=_tile(tq, ATTN_Q_SUB)),
        grid=(t // n, N_KV_HEADS, nq),
        in_specs=[pl.BlockSpec((tq, q_per_kv * LANES), lambda b, h, i: (b * nq + i, h)),
                  pl.BlockSpec((LANES, n), lambda b, h, i: (h, b)),
                  pl.BlockSpec((n, LANES), lambda b, h, i: (b, h))],
        out_specs=pl.BlockSpec((tq, q_per_kv * HEAD_DIM), lambda b, h, i: (b * nq + i, h)),
        out_shape=jax.ShapeDtypeStruct((t, d), BF16),
        compiler_params=_params("parallel", "parallel", "arbitrary"),
        name="attention",
    )(q, kt, va)


def _pool_kernel(x_ref, prev_ref, next_ref, w_ref, cb_ref, cs_ref, g_ref, b_ref, xo_ref, xb_ref,
                 ext_ref, h_ref, *, alpha, n):
    tm, d = x_ref.shape
    halo = prev_ref.shape[0]
    gdim = d // len(POOL_WINDOWS)
    pos0 = lax.rem(pl.program_id(0) * tm, n)
    x = x_ref[...]
    ext_ref[0:halo, :] = jnp.where(pos0 > 0, prev_ref[...], 0.0)
    ext_ref[halo:halo + tm, :] = x
    ext_ref[halo + tm:halo + tm + halo, :] = jnp.where(pos0 + tm < n, next_ref[...], 0.0)
    pos = pos0 + lax.broadcasted_iota(jnp.int32, (tm, 1), 0)
    for gi, w in enumerate(POOL_WINDOWS):
        cols = slice(gi * gdim, (gi + 1) * gdim)
        wsum = ext_ref[halo - w // 2:halo - w // 2 + tm, cols]
        for j in range(1 - w // 2, w - w // 2):
            wsum = wsum + ext_ref[halo + j:halo + j + tm, cols]
        cnt = (jnp.minimum(pos + (w - w // 2), n) - jnp.maximum(pos - w // 2, 0)).astype(F32)
        mix = (wsum * (1.0 / cnt) - x[:, cols]).astype(BF16)
        y = jnp.dot(mix, w_ref[gi], preferred_element_type=F32) + cb_ref[:, cols]
        h_ref[:, cols] = y * cs_ref[:, cols]
    _store_res_ln(alpha * x + h_ref[...], g_ref, b_ref, xo_ref, xb_ref)


def _pool_mixer(x, c_w, c_b, c_scale, ln_g, ln_b, alpha, n):
    t, d = x.shape
    halo = SUBLANES_F32
    assert max(POOL_WINDOWS) // 2 <= halo
    tm = _tile(n, 512)
    hb = tm // halo
    last = t // halo - 1
    row = lambda i: (i, 0)
    fixed = lambda i: (0, 0)
    gdim = d // len(POOL_WINDOWS)
    return pl.pallas_call(
        functools.partial(_pool_kernel, alpha=alpha, n=n),
        grid=(t // tm,),
        in_specs=[pl.BlockSpec((tm, d), row),
                  pl.BlockSpec((halo, d), lambda i: (jnp.maximum(i * hb - 1, 0), 0)),
                  pl.BlockSpec((halo, d), lambda i: (jnp.minimum((i + 1) * hb, last), 0)),
                  pl.BlockSpec((len(POOL_WINDOWS), gdim, gdim), lambda i: (0, 0, 0)),
                  pl.BlockSpec((1, d), fixed), pl.BlockSpec((1, d), fixed),
                  pl.BlockSpec((1, d), fixed), pl.BlockSpec((1, d), fixed)],
        out_specs=[pl.BlockSpec((tm, d), row), pl.BlockSpec((tm, d), row)],
        out_shape=[jax.ShapeDtypeStruct((t, d), F32), jax.ShapeDtypeStruct((t, d), BF16)],
        scratch_shapes=[pltpu.VMEM((tm + 2 * halo, d), F32), pltpu.VMEM((tm, d), F32)],
        compiler_params=_params("parallel"),
        name="pool_mixer",
    )(x, x, x, c_w, c_b, c_scale, ln_g, ln_b)


def _conv_kernel(h_ref, prev_ref, next_ref, x_ref, wdw_ref, bdw_ref, cg_ref, cb_ref, wo_ref, bo_ref, g_ref, b_ref,
                 xo_ref, xb_ref, ext_ref, conv_ref, *, alpha, n, rows, cols):
    tm, d = h_ref.shape
    halo = prev_ref.shape[0]
    width = wdw_ref.shape[0] // SUBLANES_F32
    pos0 = lax.rem(pl.program_id(0) * tm, n)
    ext_ref[0, 0:halo, :] = jnp.where(pos0 > 0, prev_ref[...].astype(F32), 0.0)
    ext_ref[0, halo:halo + tm, :] = h_ref[...].astype(F32)
    ext_ref[0, halo + tm:halo + tm + halo, :] = jnp.where(pos0 + tm < n, next_ref[...].astype(F32), 0.0)
    span = tm + 2 * halo - SUBLANES_F32
    for s in range(1, SUBLANES_F32):
        ext_ref[s, 0:span, :] = ext_ref[0, s:s + span, :]
    first = halo - width // 2

    sub = SUBLANES_F32

    def chunk(r, carry):
        r0 = pl.multiple_of(r * rows, rows)
        for c0 in range(0, d, cols):
            acc = [None] * (rows // sub)
            for k in range(width):
                s = (first + k) % sub
                w = wdw_ref[k * sub:(k + 1) * sub, c0:c0 + cols]
                for g in range(rows // sub):
                    tap = ext_ref[s, pl.ds(r0 + (first + k - s) + g * sub, sub), c0:c0 + cols] * w
                    acc[g] = tap if acc[g] is None else acc[g] + tap
            for g in range(rows // sub):
                conv_ref[pl.ds(r0 + g * sub, sub), c0:c0 + cols] = acc[g]
        return carry

    lax.fori_loop(0, tm // rows, chunk, 0)
    c = _silu(_ln_rows(conv_ref[...] + bdw_ref[...], cg_ref[...], cb_ref[...])).astype(BF16)
    h = jnp.dot(c, wo_ref[...], preferred_element_type=F32) + bo_ref[...]
    _store_res_ln(alpha * x_ref[...] + h, g_ref, b_ref, xo_ref, xb_ref)


def _conv_mixer(hg, x, w_dw, b_dw, c_g, c_b, w_out, b_out, ln_g, ln_b, alpha, n):
    t, d = x.shape
    halo = SUBLANES_BF16
    assert w_dw.shape[0] // SUBLANES_F32 // 2 <= halo
    tm = _tile(n, 512)
    hb = tm // halo
    last = t // halo - 1
    row = lambda i: (i, 0)
    fixed = lambda i: (0, 0)
    vec = pl.BlockSpec((1, d), fixed)
    return pl.pallas_call(
        functools.partial(_conv_kernel, alpha=alpha, n=n, rows=4 * SUBLANES_F32, cols=4 * LANES),
        grid=(t // tm,),
        in_specs=[pl.BlockSpec((tm, d), row),
                  pl.BlockSpec((halo, d), lambda i: (jnp.maximum(i * hb - 1, 0), 0)),
                  pl.BlockSpec((halo, d), lambda i: (jnp.minimum((i + 1) * hb, last), 0)),
                  pl.BlockSpec((tm, d), row),
                  pl.BlockSpec(w_dw.shape, fixed), vec, vec, vec,
                  pl.BlockSpec((d, d), fixed), vec, vec, vec],
        out_specs=[pl.BlockSpec((tm, d), row), pl.BlockSpec((tm, d), row)],
        out_shape=[jax.ShapeDtypeStruct((t, d), F32), jax.ShapeDtypeStruct((t, d), BF16)],
        scratch_shapes=[pltpu.VMEM((SUBLANES_F32, tm + 2 * halo, d), F32), pltpu.VMEM((tm, d), F32)],
        compiler_params=_params("parallel"),
        name="conv_mixer",
    )(hg, hg, hg, x, w_dw, b_dw, c_g, c_b, w_out, b_out, ln_g, ln_b)


def _router_kernel(x_ref, w_ref, b_ref, tri_ref, idx_ref, gate_ref, cnt_ref, run_ref, *, n_experts):
    @pl.when(pl.program_id(0) == 0)
    def _():
        run_ref[...] = jnp.zeros_like(run_ref)

    x = x_ref[...]
    x_hi = x.astype(BF16)
    x_lo = (x - x_hi.astype(F32)).astype(BF16)
    hi = jnp.dot(x_hi, w_ref[...], preferred_element_type=F32)
    lo = jnp.dot(x_lo, w_ref[...], preferred_element_type=F32)
    logits = hi[:, :LANES] + (hi[:, LANES:] + lo[:, :LANES]) + b_ref[...]
    lane = lax.broadcasted_iota(jnp.int32, logits.shape, 1)
    neg = jnp.float32(-jnp.inf)
    logits = jnp.where(lane < n_experts, logits, neg)
    m1 = jnp.max(logits, axis=-1, keepdims=True)
    i1 = jnp.min(jnp.where(logits == m1, lane, LANES), axis=-1, keepdims=True)
    rest = jnp.where(lane == i1, neg, logits)
    m2 = jnp.max(rest, axis=-1, keepdims=True)
    i2 = jnp.min(jnp.where(rest == m2, lane, LANES), axis=-1, keepdims=True)
    e2 = jnp.exp(m2 - m1)
    den = 1.0 + e2
    gate_ref[...] = jnp.where(lane == 0, 1.0 / den, jnp.where(lane == 1, e2 / den, 0.0))
    chosen = jnp.where((lane == i1) | (lane == i2), 1.0, 0.0)
    before = jnp.dot(tri_ref[...], chosen.astype(BF16), preferred_element_type=F32) + run_ref[...]
    r1 = jnp.sum(jnp.where(lane == i1, before, 0.0), axis=-1, keepdims=True).astype(jnp.int32)
    r2 = jnp.sum(jnp.where(lane == i2, before, 0.0), axis=-1, keepdims=True).astype(jnp.int32)
    idx_ref[...] = jnp.where(lane == 0, i1, jnp.where(lane == 1, i2, jnp.where(lane == 2, r1, jnp.where(lane == 3, r2, 0))))
    run = run_ref[...] + jnp.sum(chosen, axis=0, keepdims=True)
    run_ref[...] = run
    cnt_ref[...] = run.astype(jnp.int32)


def _router(x, w_r, b_r):
    t, d = x.shape
    n_experts = w_r.shape[1]
    w_pad = jnp.zeros((d, LANES), F32).at[:, :n_experts].set(w_r)
    w_hi = w_pad.astype(BF16)
    w_split = jnp.concatenate([w_hi, (w_pad - w_hi.astype(F32)).astype(BF16)], axis=1)
    b_pad = jnp.zeros((1, LANES), F32).at[0, :n_experts].set(b_r)
    tm = _tile(t, 1024)
    tri = (jnp.arange(tm)[:, None] > jnp.arange(tm)[None, :]).astype(BF16)
    row = lambda i: (i, 0)
    fixed = lambda i: (0, 0)
    return pl.pallas_call(
        functools.partial(_router_kernel, n_experts=n_experts),
        grid=(t // tm,),
        in_specs=[pl.BlockSpec((tm, d), row), pl.BlockSpec((d, 2 * LANES), fixed), pl.BlockSpec((1, LANES), fixed),
                  pl.BlockSpec((tm, tm), fixed)],
        out_specs=[pl.BlockSpec((tm, LANES), row), pl.BlockSpec((tm, LANES), row), pl.BlockSpec((1, LANES), fixed)],
        out_shape=[jax.ShapeDtypeStruct((t, LANES), jnp.int32), jax.ShapeDtypeStruct((t, LANES), F32),
                   jax.ShapeDtypeStruct((1, LANES), jnp.int32)],
        scratch_shapes=[pltpu.VMEM((1, LANES), F32)],
        compiler_params=_params("arbitrary"),
        name="router",
    )(x, w_split, b_pad, tri)


def _moe_dispatch(idx, counts, n_experts, tm):
    t = idx.shape[0]
    s = t * TOP_K
    padded = ((counts + tm - 1) // tm) * tm
    pend = jnp.cumsum(padded)
    pstart = pend - padded
    onehot = idx[:, :TOP_K, None] == jnp.arange(n_experts, dtype=jnp.int32)
    dest = jnp.sum(jnp.where(onehot, pstart, 0), axis=-1) + idx[:, TOP_K:2 * TOP_K]
    n_tiles = s // tm + n_experts
    tok = jnp.broadcast_to(jnp.arange(t, dtype=jnp.int32)[:, None], (t, TOP_K))
    src_tok = jnp.zeros((n_tiles * tm,), jnp.int32).at[dest.reshape(s)].set(
        tok.reshape(s), unique_indices=True, mode="promise_in_bounds")
    nv = (pend[-1] // tm).astype(jnp.int32)
    tile_idx = jnp.arange(n_tiles, dtype=jnp.int32)
    te = jnp.searchsorted(pend, jnp.minimum(tile_idx, nv - 1) * tm, side="right").astype(jnp.int32)
    return src_tok, te, nv.reshape(1), dest


def _moe_layer(x, xb, w_r, b_r, w1, w3, w2, j, ln_g, ln_b, alpha):
    t, d = x.shape
    n_experts = w_r.shape[1]
    tm = _tile(t * TOP_K, 512)
    idx, gate, counts = _router(x, w_r, b_r)
    src_tok, te, nv, dest = _moe_dispatch(idx, counts[0, :n_experts], n_experts, tm)
    rows = lambda a, r: a.at[r].get(mode="promise_in_bounds")
    xs = rows(xb, src_tok)
    out_p = _ffn_moe(xs, te + j * n_experts, nv, w1, w3, w2, tm)
    o0 = rows(out_p, dest[:, 0])
    o1 = rows(out_p, dest[:, 1])
    return _add_res_ln(x, o0, o1, gate, ln_g, ln_b, alpha)


def kernel(x_prompt, x_sample, ln_mix_g, ln_mix_b, ln_ffn_g, ln_ffn_b, a_w_in, a_b_in, a_ln_g, a_ln_b, a_w_s, a_b_s, a_w_out, b_w_qkv, b_q_g, b_k_g, b_w_o, c_w, c_b, c_scale, d_w_in, d_b_in, d_w_dw, d_b_dw, d_ln_g, d_ln_b, d_w_out, d_b_out, f_w1, f_w3, f_w2, m_w_r, m_b_r, m_w1, m_w3, m_w2):
    d = x_prompt.shape[-1]
    depth = ln_mix_g.shape[0]
    alpha = (2.0 * depth) ** 0.25
    vec = lambda v: v.reshape(1, -1).astype(F32)
    two_heads = lambda g: jnp.tile(g.astype(F32), 2).reshape(1, LANES)

    a_w_in, a_w_out, b_w_qkv, b_w_o, d_w_in, d_w_out = map(_to_bf16, (a_w_in, a_w_out, b_w_qkv, b_w_o, d_w_in, d_w_out))
    f_w1, f_w3, f_w2 = map(_to_bf16, (f_w1, f_w3, f_w2))
    n_moe, n_experts = m_w1.shape[:2]
    m_w1, m_w3, m_w2 = (_to_bf16(w).reshape((n_moe * n_experts,) + w.shape[2:]) for w in (m_w1, m_w3, m_w2))
    a_w_s, c_w = a_w_s.astype(BF16), c_w.astype(BF16)
    d_w_dw = jnp.repeat(d_w_dw.astype(F32), SUBLANES_F32, axis=1)

    groups = []
    for xin in (x_prompt, x_sample):
        bsz, n, _ = xin.shape
        x = xin.reshape(bsz * n, d)
        groups.append([x, _to_bf16(x), n])
    for l in range(depth):
        kind, i = l % 4, l // 4
        mg, mb = vec(ln_mix_g[l]), vec(ln_mix_b[l])
        fg, fb = vec(ln_ffn_g[l]), vec(ln_ffn_b[l])
        j = l // 2
        for grp in groups:
            x, xb, n = grp
            if kind == 0:
                z = _mm_gelu(xb, a_w_in[i], vec(a_b_in[i]))
                x, xb = _gmlp_gate(z, x, vec(a_ln_g[i]), vec(a_ln_b[i]), a_w_s[i], a_b_s[i].T.astype(F32),
                                   a_w_out[i], mg, mb, alpha)
            elif kind == 1:
                q, kt, va = _qkv(xb, b_w_qkv[i], two_heads(b_q_g[i]), two_heads(b_k_g[i]), n)
                o = _attention(q, kt, va, n, d)
                x, xb = _mm_res_ln(o, b_w_o[i], x, mg, mb, alpha)
            elif kind == 2:
                x, xb = _pool_mixer(x, c_w[i], vec(c_b[i]), vec(c_scale[i]), mg, mb, alpha, n)
            else:
                hg = _mm_glu(xb, d_w_in[i], vec(d_b_in[i]))
                x, xb = _conv_mixer(hg, x, d_w_dw[i], vec(d_b_dw[i]), vec(d_ln_g[i]), vec(d_ln_b[i]),
                                    d_w_out[i], vec(d_b_out[i]), mg, mb, alpha, n)
            if l % 2 == 0:
                x, xb = _ffn_dense(xb, x, f_w1, f_w3, f_w2, j, fg, fb, alpha)
            else:
                x, xb = _moe_layer(x, xb, m_w_r[j], m_b_r[j], m_w1, m_w3, m_w2, j, fg, fb, alpha)
            grp[0], grp[1] = x, xb
    return tuple(x.reshape(xin.shape) for (x, _, _), xin in zip(groups, (x_prompt, x_sample)))
```

```python
import functools
import math

import jax
import jax.numpy as jnp
from jax import lax
from jax.experimental import pallas as pl
from jax.experimental.pallas import tpu as pltpu

F32 = jnp.float32
BF16 = jnp.bfloat16

GRID_W = 64
BLOCK = 128
A_GROUPS = 8
HEAD_DIM = 64
N_KV_HEADS = 4
ROPE_THETA = 10000.0
POOL_WINDOWS = (2, 4, 8, 16)
TOP_K = 2
LN_EPS = 1e-5
RMS_EPS = 1e-6

LANES = 128
SUBLANES_F32 = 8
SUBLANES_BF16 = 16
VMEM_LIMIT = 56 * 1024 * 1024
FFN_CHUNK = 512
ATTN_STEP_SCORES = 512 * 4096
ATTN_Q_SUB = 256


def _params(*sem):
    return pltpu.CompilerParams(dimension_semantics=sem, vmem_limit_bytes=VMEM_LIMIT)


def _tile(total, target):
    t = math.gcd(total, target)
    assert t % SUBLANES_BF16 == 0, (total, target)
    return t


def _ln_rows(y, g, b):
    mu = jnp.mean(y, axis=-1, keepdims=True)
    yc = y - mu
    var = jnp.mean(yc * yc, axis=-1, keepdims=True)
    return yc * lax.rsqrt(var + LN_EPS) * g + b


def _gelu_tanh(x):
    c = math.sqrt(2.0 / math.pi)
    return x * (0.5 * (1.0 + jnp.tanh(c * (x + 0.044715 * (x * x * x)))))


def _silu(x):
    return x * jax.nn.sigmoid(x)


def _cast_kernel(x_ref, o_ref):
    o_ref[...] = x_ref[...].astype(o_ref.dtype)


def _to_bf16(w):
    cols = w.shape[-1]
    rows = w.size // cols
    tm = _tile(rows, 512)
    out = pl.pallas_call(
        _cast_kernel,
        grid=(rows // tm,),
        in_specs=[pl.BlockSpec((tm, cols), lambda i: (i, 0))],
        out_specs=pl.BlockSpec((tm, cols), lambda i: (i, 0)),
        out_shape=jax.ShapeDtypeStruct((rows, cols), BF16),
        compiler_params=_params("parallel"),
        name="to_bf16",
    )(w.reshape(rows, cols))
    return out.reshape(w.shape)


def _store_res_ln(y, g_ref, b_ref, xo_ref, xb_ref):
    out = _ln_rows(y, g_ref[...], b_ref[...])
    xo_ref[...] = out
    xb_ref[...] = out.astype(BF16)


def _mm_gelu_kernel(x_ref, w_ref, b_ref, o_ref, *, sub):
    x = x_ref[...]
    for c0 in range(0, o_ref.shape[1], sub):
        z = jnp.dot(x, w_ref[:, c0:c0 + sub], preferred_element_type=F32) + b_ref[:, c0:c0 + sub]
        o_ref[:, c0:c0 + sub] = _gelu_tanh(z).astype(o_ref.dtype)


def _mm_gelu(xb, w, b):
    t, k = xb.shape
    n = w.shape[1]
    tm, tn = _tile(t, 1024), _tile(n, 4096)
    return pl.pallas_call(
        functools.partial(_mm_gelu_kernel, sub=_tile(tn, 1024)),
        grid=(n // tn, t // tm),
        in_specs=[pl.BlockSpec((tm, k), lambda j, i: (i, 0)),
                  pl.BlockSpec((k, tn), lambda j, i: (0, j)),
                  pl.BlockSpec((1, tn), lambda j, i: (0, j))],
        out_specs=pl.BlockSpec((tm, tn), lambda j, i: (i, j)),
        out_shape=jax.ShapeDtypeStruct((t, n), BF16),
        compiler_params=_params("parallel", "parallel"),
        name="mm_gelu",
    )(xb, w, b)


def _mm_glu_kernel(x_ref, wa_ref, wg_ref, ba_ref, bg_ref, o_ref):
    x = x_ref[...]
    a = jnp.dot(x, wa_ref[...], preferred_element_type=F32) + ba_ref[...]
    g = jnp.dot(x, wg_ref[...], preferred_element_type=F32) + bg_ref[...]
    o_ref[...] = (a * jax.nn.sigmoid(g)).astype(o_ref.dtype)


def _mm_glu(xb, w, b):
    t, k = xb.shape
    n = w.shape[1] // 2
    tm, tn = _tile(t, 1024), _tile(n, 512)
    nj = n // tn
    return pl.pallas_call(
        _mm_glu_kernel,
        grid=(nj, t // tm),
        in_specs=[pl.BlockSpec((tm, k), lambda j, i: (i, 0)),
                  pl.BlockSpec((k, tn), lambda j, i: (0, j)),
                  pl.BlockSpec((k, tn), lambda j, i: (0, j + nj)),
                  pl.BlockSpec((1, tn), lambda j, i: (0, j)),
                  pl.BlockSpec((1, tn), lambda j, i: (0, j + nj))],
        out_specs=pl.BlockSpec((tm, tn), lambda j, i: (i, j)),
        out_shape=jax.ShapeDtypeStruct((t, n), BF16),
        compiler_params=_params("parallel", "parallel"),
        name="mm_glu",
    )(xb, w, w, b, b)


def _gmlp_gate_kernel(z_ref, x_ref, vg_ref, vb_ref, ws_ref, bs_ref, wo_ref, g_ref, b_ref,
                      xo_ref, xb_ref, gated_ref, *, alpha, half, gdim):
    tm = z_ref.shape[0]
    v = _ln_rows(z_ref[:, half:].astype(F32), vg_ref[...], vb_ref[...]).astype(BF16)
    for c in range(tm // BLOCK):
        rows = slice(c * BLOCK, (c + 1) * BLOCK)
        for gi in range(A_GROUPS):
            cols = slice(gi * gdim, (gi + 1) * gdim)
            sv = jnp.dot(ws_ref[gi], v[rows, cols], preferred_element_type=F32) + bs_ref[:, gi:gi + 1]
            gated_ref[rows, cols] = (z_ref[rows, cols].astype(F32) * sv).astype(BF16)
    h = jnp.dot(gated_ref[...], wo_ref[...], preferred_element_type=F32)
    _store_res_ln(alpha * x_ref[...] + h, g_ref, b_ref, xo_ref, xb_ref)


def _gmlp_gate(z, x, v_g, v_b, w_s, b_s_t, w_out, ln_g, ln_b, alpha):
    t, d = x.shape
    half = z.shape[1] // 2
    tm = _tile(t, 4 * BLOCK)
    row = lambda i: (i, 0)
    fixed2 = lambda i: (0, 0)
    return pl.pallas_call(
        functools.partial(_gmlp_gate_kernel, alpha=alpha, half=half, gdim=half // A_GROUPS),
        grid=(t // tm,),
        in_specs=[pl.BlockSpec((tm, 2 * half), row),
                  pl.BlockSpec((tm, d), row),
                  pl.BlockSpec((1, half), fixed2),
                  pl.BlockSpec((1, half), fixed2),
                  pl.BlockSpec((A_GROUPS, BLOCK, BLOCK), lambda i: (0, 0, 0)),
                  pl.BlockSpec((BLOCK, A_GROUPS), fixed2),
                  pl.BlockSpec((half, d), fixed2),
                  pl.BlockSpec((1, d), fixed2),
                  pl.BlockSpec((1, d), fixed2)],
        out_specs=[pl.BlockSpec((tm, d), row), pl.BlockSpec((tm, d), row)],
        out_shape=[jax.ShapeDtypeStruct((t, d), F32), jax.ShapeDtypeStruct((t, d), BF16)],
        scratch_shapes=[pltpu.VMEM((tm, half), BF16)],
        compiler_params=_params("parallel"),
        name="gmlp_gate",
    )(z, x, v_g, v_b, w_s, b_s_t, w_out, ln_g, ln_b)


def _swiglu(x, w1_ref, w3_ref, w2_ref, tf):
    acc = None
    for c0 in range(0, w1_ref.shape[2], tf):
        a = jnp.dot(x, w1_ref[0, :, c0:c0 + tf], preferred_element_type=F32)
        b = jnp.dot(x, w3_ref[0, :, c0:c0 + tf], preferred_element_type=F32)
        mid = (_silu(a) * b).astype(BF16)
        part = jnp.dot(mid, w2_ref[0, c0:c0 + tf, :], preferred_element_type=F32)
        acc = part if acc is None else acc + part
    return acc


def _ffn_dense_kernel(x_ref, w1_ref, w3_ref, w2_ref, r_ref, g_ref, b_ref, xo_ref, xb_ref, *, alpha, tf):
    h = _swiglu(x_ref[...], w1_ref, w3_ref, w2_ref, tf)
    _store_res_ln(alpha * r_ref[...] + h, g_ref, b_ref, xo_ref, xb_ref)


def _ffn_moe_kernel(te_ref, nv_ref, x_ref, w1_ref, w3_ref, w2_ref, o_ref, *, tf):
    valid = pl.program_id(0) < nv_ref[0]

    @pl.when(valid)
    def _():
        o_ref[...] = _swiglu(x_ref[...], w1_ref, w3_ref, w2_ref, tf).astype(o_ref.dtype)

    @pl.when(jnp.logical_not(valid))
    def _():
        o_ref[...] = jnp.zeros_like(o_ref)


def _ffn_dense(xb, x, w1, w3, w2, j, ln_g, ln_b, alpha):
    t, d = x.shape
    f = w1.shape[2]
    tm, tf = _tile(t, 512), _tile(f, FFN_CHUNK)
    row = lambda i: (i, 0)
    fixed = lambda i: (0, 0)
    layer = lambda i: (j, 0, 0)
    once = pl.Buffered(1)
    return pl.pallas_call(
        functools.partial(_ffn_dense_kernel, alpha=alpha, tf=tf),
        grid=(t // tm,),
        in_specs=[pl.BlockSpec((tm, d), row),
                  pl.BlockSpec((1, d, f), layer, pipeline_mode=once),
                  pl.BlockSpec((1, d, f), layer, pipeline_mode=once),
                  pl.BlockSpec((1, f, d), layer, pipeline_mode=once),
                  pl.BlockSpec((tm, d), row), pl.BlockSpec((1, d), fixed), pl.BlockSpec((1, d), fixed)],
        out_specs=[pl.BlockSpec((tm, d), row), pl.BlockSpec((tm, d), row)],
        out_shape=[jax.ShapeDtypeStruct((t, d), F32), jax.ShapeDtypeStruct((t, d), BF16)],
        compiler_params=_params("parallel"),
        name="ffn_dense",
    )(xb, w1, w3, w2, x, ln_g, ln_b)


def _ffn_moe(xs, te, nv, w1, w3, w2, tm):
    p, d = xs.shape
    f = w1.shape[2]
    expert = lambda i, te, nv: (te[i], 0, 0)
    once = pl.Buffered(1)
    return pl.pallas_call(
        functools.partial(_ffn_moe_kernel, tf=_tile(f, FFN_CHUNK)),
        grid_spec=pltpu.PrefetchScalarGridSpec(
            num_scalar_prefetch=2, grid=(p // tm,),
            in_specs=[pl.BlockSpec((tm, d), lambda i, te, nv: (jnp.minimum(i, nv[0] - 1), 0)),
                      pl.BlockSpec((1, d, f), expert, pipeline_mode=once),
                      pl.BlockSpec((1, d, f), expert, pipeline_mode=once),
                      pl.BlockSpec((1, f, d), expert, pipeline_mode=once)],
            out_specs=pl.BlockSpec((tm, d), lambda i, te, nv: (i, 0))),
        out_shape=jax.ShapeDtypeStruct((p, d), BF16),
        compiler_params=_params("arbitrary"),
        name="ffn_moe",
    )(te, nv, xs, w1, w3, w2)


def _mm_res_ln_kernel(a_ref, w_ref, r_ref, g_ref, b_ref, xo_ref, xb_ref, *, alpha):
    h = jnp.dot(a_ref[...], w_ref[...], preferred_element_type=F32)
    _store_res_ln(alpha * r_ref[...] + h, g_ref, b_ref, xo_ref, xb_ref)


def _mm_res_ln(a, w, x, ln_g, ln_b, alpha):
    t, d = x.shape
    k = a.shape[1]
    tm = _tile(t, 512)
    row = lambda i: (i, 0)
    fixed = lambda i: (0, 0)
    return pl.pallas_call(
        functools.partial(_mm_res_ln_kernel, alpha=alpha),
        grid=(t // tm,),
        in_specs=[pl.BlockSpec((tm, k), row), pl.BlockSpec((k, d), fixed), pl.BlockSpec((tm, d), row),
                  pl.BlockSpec((1, d), fixed), pl.BlockSpec((1, d), fixed)],
        out_specs=[pl.BlockSpec((tm, d), row), pl.BlockSpec((tm, d), row)],
        out_shape=[jax.ShapeDtypeStruct((t, d), F32), jax.ShapeDtypeStruct((t, d), BF16)],
        compiler_params=_params("parallel"),
        name="mm_res_ln",
    )(a, w, x, ln_g, ln_b)


def _add_res_ln_kernel(r_ref, o0_ref, o1_ref, gate_ref, g_ref, b_ref, xo_ref, xb_ref, *, alpha):
    h = gate_ref[:, 0:1] * o0_ref[...].astype(F32) + gate_ref[:, 1:2] * o1_ref[...].astype(F32)
    _store_res_ln(alpha * r_ref[...] + h, g_ref, b_ref, xo_ref, xb_ref)


def _add_res_ln(x, o0, o1, gate, ln_g, ln_b, alpha):
    t, d = x.shape
    tm = _tile(t, 1024)
    row = lambda i: (i, 0)
    fixed = lambda i: (0, 0)
    return pl.pallas_call(
        functools.partial(_add_res_ln_kernel, alpha=alpha),
        grid=(t // tm,),
        in_specs=[pl.BlockSpec((tm, d), row), pl.BlockSpec((tm, d), row), pl.BlockSpec((tm, d), row),
                  pl.BlockSpec((tm, LANES), row), pl.BlockSpec((1, d), fixed), pl.BlockSpec((1, d), fixed)],
        out_specs=[pl.BlockSpec((tm, d), row), pl.BlockSpec((tm, d), row)],
        out_shape=[jax.ShapeDtypeStruct((t, d), F32), jax.ShapeDtypeStruct((t, d), BF16)],
        compiler_params=_params("parallel"),
        name="add_res_ln",
    )(x, o0, o1, gate, ln_g, ln_b)


def _qkv_kernel(x_ref, w_ref, qg_ref, kg_ref, cos_ref, sa_ref, sb_ref, q_ref, kt_ref, v_ref, *, n_q):
    tm = x_ref.shape[0]
    acc = jnp.dot(x_ref[...], w_ref[...], preferred_element_type=F32)
    lane = lax.broadcasted_iota(jnp.int32, (tm, LANES), 1)
    left = lane < HEAD_DIM
    cos, sin_a, sin_b = cos_ref[...], sa_ref[...], sb_ref[...]

    def norm_rope(c, g, scale):
        s = c * c
        ss_l = jnp.sum(jnp.where(left, s, 0.0), axis=-1, keepdims=True)
        ss_r = jnp.sum(jnp.where(left, 0.0, s), axis=-1, keepdims=True)
        r = lax.rsqrt(jnp.where(left, ss_l, ss_r) * (1.0 / HEAD_DIM) + RMS_EPS)
        cn = c * r * g
        y = cn * cos + pltpu.roll(cn, LANES - HEAD_DIM // 4, 1) * sin_a + pltpu.roll(cn, HEAD_DIM // 4, 1) * sin_b
        return y * scale

    def split_heads(y, fill):
        return jnp.where(left, y, fill), jnp.where(left, pltpu.roll(y, HEAD_DIM, 1), fill)

    zero = jnp.zeros((tm, LANES), F32)
    for j in range(n_q // 2):
        y = norm_rope(acc[:, j * LANES:(j + 1) * LANES], qg_ref[...], HEAD_DIM ** -0.5)
        even, odd = split_heads(y, zero)
        q_ref[:, (2 * j) * LANES:(2 * j + 1) * LANES] = even.astype(BF16)
        q_ref[:, (2 * j + 1) * LANES:(2 * j + 2) * LANES] = odd.astype(BF16)
    k_off = n_q * HEAD_DIM
    v_off = k_off + N_KV_HEADS * HEAD_DIM
    ones_col = jnp.where(lane == HEAD_DIM, 1.0, 0.0)
    for j in range(N_KV_HEADS // 2):
        y = norm_rope(acc[:, k_off + j * LANES:k_off + (j + 1) * LANES], kg_ref[...], 1.0)
        even, odd = split_heads(y, zero)
        kt_ref[(2 * j) * LANES:(2 * j + 1) * LANES, :] = even.T.astype(BF16)
        kt_ref[(2 * j + 1) * LANES:(2 * j + 2) * LANES, :] = odd.T.astype(BF16)
        even, odd = split_heads(acc[:, v_off + j * LANES:v_off + (j + 1) * LANES], ones_col)
        v_ref[:, (2 * j) * LANES:(2 * j + 1) * LANES] = even.astype(BF16)
        v_ref[:, (2 * j + 1) * LANES:(2 * j + 2) * LANES] = odd.astype(BF16)


def _rope_tables(n):
    pos = jnp.arange(n)
    row = (pos // GRID_W).astype(F32)
    col = (pos % GRID_W).astype(F32)
    sec = HEAD_DIM // 2
    inv = ROPE_THETA ** (-jnp.arange(0, sec, 2, dtype=F32) / sec)
    ang_r = row[:, None] * inv[None, :]
    ang_c = col[:, None] * inv[None, :]
    ang = jnp.concatenate([ang_r, ang_r, ang_c, ang_c], axis=-1)
    ang = jnp.concatenate([ang, ang], axis=-1)
    first = (jnp.arange(LANES) % (HEAD_DIM // 2)) < HEAD_DIM // 4
    sin = jnp.sin(ang)
    return jnp.cos(ang), jnp.where(first, -sin, 0.0), jnp.where(first, 0.0, sin)


def _qkv(xb, w_qkv, q_g2, k_g2, n):
    t, d = xb.shape
    n_q = d // HEAD_DIM
    tm = _tile(n, 512)
    cos, sin_a, sin_b = _rope_tables(n)
    tab = lambda i: (lax.rem(i, n // tm), 0)
    row = lambda i: (i, 0)
    fixed = lambda i: (0, 0)
    return pl.pallas_call(
        functools.partial(_qkv_kernel, n_q=n_q),
        grid=(t // tm,),
        in_specs=[pl.BlockSpec((tm, d), row), pl.BlockSpec(w_qkv.shape, fixed),
                  pl.BlockSpec((1, LANES), fixed), pl.BlockSpec((1, LANES), fixed),
                  pl.BlockSpec((tm, LANES), tab), pl.BlockSpec((tm, LANES), tab), pl.BlockSpec((tm, LANES), tab)],
        out_specs=[pl.BlockSpec((tm, n_q * LANES), row),
                   pl.BlockSpec((N_KV_HEADS * LANES, tm), lambda i: (0, i)),
                   pl.BlockSpec((tm, N_KV_HEADS * LANES), row)],
        out_shape=[jax.ShapeDtypeStruct((t, n_q * LANES), BF16),
                   jax.ShapeDtypeStruct((N_KV_HEADS * LANES, t), BF16),
                   jax.ShapeDtypeStruct((t, N_KV_HEADS * LANES), BF16)],
        compiler_params=_params("parallel"),
        name="qkv_rope",
    )(xb, w_qkv, q_g2, k_g2, cos, sin_a, sin_b)


def _attn_kernel(q_ref, kt_ref, v_ref, o_ref, *, q_per_kv, sub):
    kt = kt_ref[...]
    va = v_ref[...]
    left = lax.broadcasted_iota(jnp.int32, (sub, LANES), 1) < HEAD_DIM
    for r0 in range(0, q_ref.shape[0], sub):
        outs = []
        for h in range(q_per_kv):
            s = jnp.dot(q_ref[r0:r0 + sub, h * LANES:(h + 1) * LANES], kt, preferred_element_type=F32)
            m = jnp.max(s, axis=-1, keepdims=True)
            p = jnp.exp(s - m).astype(BF16)
            o = jnp.dot(p, va, preferred_element_type=F32)
            outs.append(o * (1.0 / o[:, HEAD_DIM:HEAD_DIM + 1]))
        for j in range(q_per_kv // 2):
            pair = jnp.where(left, outs[2 * j], pltpu.roll(outs[2 * j + 1], HEAD_DIM, 1))
            o_ref[r0:r0 + sub, j * LANES:(j + 1) * LANES] = pair.astype(BF16)


def _attention(q, kt, va, n, d):
    t = q.shape[0]
    q_per_kv = d // HEAD_DIM // N_KV_HEADS
    tq = _tile(n, max(ATTN_Q_SUB, ATTN_STEP_SCORES // n))
    nq = n // tq
    return pl.pallas_call(
        functools.partial(_attn_kernel, q_per_kv=q_per_kv, sub=_tile(tq, ATTN_Q_SUB)),
        grid=(t // n, N_KV_HEADS, nq),
        in_specs=[pl.BlockSpec((tq, q_per_kv * LANES), lambda b, h, i: (b * nq + i, h)),
                  pl.BlockSpec((LANES, n), lambda b, h, i: (h, b)),
                  pl.BlockSpec((n, LANES), lambda b, h, i: (b, h))],
        out_specs=pl.BlockSpec((tq, q_per_kv * HEAD_DIM), lambda b, h, i: (b * nq + i, h)),
        out_shape=jax.ShapeDtypeStruct((t, d), BF16),
        compiler_params=_params("parallel", "parallel", "arbitrary"),
        name="attention",
    )(q, kt, va)


def _pool_kernel(x_ref, prev_ref, next_ref, w_ref, cb_ref, cs_ref, g_ref, b_ref, xo_ref, xb_ref,
                 ext_ref, h_ref, *, alpha, n):
    tm, d = x_ref.shape
    halo = prev_ref.shape[0]
    gdim = d // len(POOL_WINDOWS)
    pos0 = lax.rem(pl.program_id(0) * tm, n)
    x = x_ref[...]
    ext_ref[0:halo, :] = jnp.where(pos0 > 0, prev_ref[...], 0.0)
    ext_ref[halo:halo + tm, :] = x
    ext_ref[halo + tm:halo + tm + halo, :] = jnp.where(pos0 + tm < n, next_ref[...], 0.0)
    pos = pos0 + lax.broadcasted_iota(jnp.int32, (tm, 1), 0)
    for gi, w in enumerate(POOL_WINDOWS):
        cols = slice(gi * gdim, (gi + 1) * gdim)
        wsum = ext_ref[halo - w // 2:halo - w // 2 + tm, cols]
        for j in range(1 - w // 2, w - w // 2):
            wsum = wsum + ext_ref[halo + j:halo + j + tm, cols]
        cnt = (jnp.minimum(pos + (w - w // 2), n) - jnp.maximum(pos - w // 2, 0)).astype(F32)
        mix = (wsum * (1.0 / cnt) - x[:, cols]).astype(BF16)
        y = jnp.dot(mix, w_ref[gi], preferred_element_type=F32) + cb_ref[:, cols]
        h_ref[:, cols] = y * cs_ref[:, cols]
    _store_res_ln(alpha * x + h_ref[...], g_ref, b_ref, xo_ref, xb_ref)


def _pool_mixer(x, c_w, c_b, c_scale, ln_g, ln_b, alpha, n):
    t, d = x.shape
    halo = SUBLANES_F32
    assert max(POOL_WINDOWS) // 2 <= halo
    tm = _tile(n, 512)
    hb = tm // halo
    last = t // halo - 1
    row = lambda i: (i, 0)
    fixed = lambda i: (0, 0)
    gdim = d // len(POOL_WINDOWS)
    return pl.pallas_call(
        functools.partial(_pool_kernel, alpha=alpha, n=n),
        grid=(t // tm,),
        in_specs=[pl.BlockSpec((tm, d), row),
                  pl.BlockSpec((halo, d), lambda i: (jnp.maximum(i * hb - 1, 0), 0)),
                  pl.BlockSpec((halo, d), lambda i: (jnp.minimum((i + 1) * hb, last), 0)),
                  pl.BlockSpec((len(POOL_WINDOWS), gdim, gdim), lambda i: (0, 0, 0)),
                  pl.BlockSpec((1, d), fixed), pl.BlockSpec((1, d), fixed),
                  pl.BlockSpec((1, d), fixed), pl.BlockSpec((1, d), fixed)],
        out_specs=[pl.BlockSpec((tm, d), row), pl.BlockSpec((tm, d), row)],
        out_shape=[jax.ShapeDtypeStruct((t, d), F32), jax.ShapeDtypeStruct((t, d), BF16)],
        scratch_shapes=[pltpu.VMEM((tm + 2 * halo, d), F32), pltpu.VMEM((tm, d), F32)],
        compiler_params=_params("parallel"),
        name="pool_mixer",
    )(x, x, x, c_w, c_b, c_scale, ln_g, ln_b)


def _conv_kernel(h_ref, prev_ref, next_ref, x_ref, wdw_ref, bdw_ref, cg_ref, cb_ref, wo_ref, bo_ref, g_ref, b_ref,
                 xo_ref, xb_ref, ext_ref, conv_ref, *, alpha, n, rows, cols):
    tm, d = h_ref.shape
    halo = prev_ref.shape[0]
    width = wdw_ref.shape[0] // SUBLANES_F32
    pos0 = lax.rem(pl.program_id(0) * tm, n)
    ext_ref[0, 0:halo, :] = jnp.where(pos0 > 0, prev_ref[...].astype(F32), 0.0)
    ext_ref[0, halo:halo + tm, :] = h_ref[...].astype(F32)
    ext_ref[0, halo + tm:halo + tm + halo, :] = jnp.where(pos0 + tm < n, next_ref[...].astype(F32), 0.0)
    span = tm + 2 * halo - SUBLANES_F32
    for s in range(1, SUBLANES_F32):
        ext_ref[s, 0:span, :] = ext_ref[0, s:s + span, :]
    first = halo - width // 2

    sub = SUBLANES_F32

    def chunk(r, carry):
        r0 = pl.multiple_of(r * rows, rows)
        for c0 in range(0, d, cols):
            acc = [None] * (rows // sub)
            for k in range(width):
                s = (first + k) % sub
                w = wdw_ref[k * sub:(k + 1) * sub, c0:c0 + cols]
                for g in range(rows // sub):
                    tap = ext_ref[s, pl.ds(r0 + (first + k - s) + g * sub, sub), c0:c0 + cols] * w
                    acc[g] = tap if acc[g] is None else acc[g] + tap
            for g in range(rows // sub):
                conv_ref[pl.ds(r0 + g * sub, sub), c0:c0 + cols] = acc[g]
        return carry

    lax.fori_loop(0, tm // rows, chunk, 0)
    c = _silu(_ln_rows(conv_ref[...] + bdw_ref[...], cg_ref[...], cb_ref[...])).astype(BF16)
    h = jnp.dot(c, wo_ref[...], preferred_element_type=F32) + bo_ref[...]
    _store_res_ln(alpha * x_ref[...] + h, g_ref, b_ref, xo_ref, xb_ref)


def _conv_mixer(hg, x, w_dw, b_dw, c_g, c_b, w_out, b_out, ln_g, ln_b, alpha, n):
    t, d = x.shape
    halo = SUBLANES_BF16
    assert w_dw.shape[0] // SUBLANES_F32 // 2 <= halo
    tm = _tile(n, 512)
    hb = tm // halo
    last = t // halo - 1
    row = lambda i: (i, 0)
    fixed = lambda i: (0, 0)
    vec = pl.BlockSpec((1, d), fixed)
    return pl.pallas_call(
        functools.partial(_conv_kernel, alpha=alpha, n=n, rows=4 * SUBLANES_F32, cols=4 * LANES),
        grid=(t // tm,),
        in_specs=[pl.BlockSpec((tm, d), row),
                  pl.BlockSpec((halo, d), lambda i: (jnp.maximum(i * hb - 1, 0), 0)),
                  pl.BlockSpec((halo, d), lambda i: (jnp.minimum((i + 1) * hb, last), 0)),
                  pl.BlockSpec((tm, d), row),
                  pl.BlockSpec(w_dw.shape, fixed), vec, vec, vec,
                  pl.BlockSpec((d, d), fixed), vec, vec, vec],
        out_specs=[pl.BlockSpec((tm, d), row), pl.BlockSpec((tm, d), row)],
        out_shape=[jax.ShapeDtypeStruct((t, d), F32), jax.ShapeDtypeStruct((t, d), BF16)],
        scratch_shapes=[pltpu.VMEM((SUBLANES_F32, tm + 2 * halo, d), F32), pltpu.VMEM((tm, d), F32)],
        compiler_params=_params("parallel"),
        name="conv_mixer",
    )(hg, hg, hg, x, w_dw, b_dw, c_g, c_b, w_out, b_out, ln_g, ln_b)


def _router_kernel(x_ref, w_ref, b_ref, tri_ref, idx_ref, gate_ref, cnt_ref, run_ref, *, n_experts):
    @pl.when(pl.program_id(0) == 0)
    def _():
        run_ref[...] = jnp.zeros_like(run_ref)

    x = x_ref[...]
    x_hi = x.astype(BF16)
    x_lo = (x - x_hi.astype(F32)).astype(BF16)
    hi = jnp.dot(x_hi, w_ref[...], preferred_element_type=F32)
    lo = jnp.dot(x_lo, w_ref[...], preferred_element_type=F32)
    logits = hi[:, :LANES] + (hi[:, LANES:] + lo[:, :LANES]) + b_ref[...]
    lane = lax.broadcasted_iota(jnp.int32, logits.shape, 1)
    neg = jnp.float32(-jnp.inf)
    logits = jnp.where(lane < n_experts, logits, neg)
    m1 = jnp.max(logits, axis=-1, keepdims=True)
    i1 = jnp.min(jnp.where(logits == m1, lane, LANES), axis=-1, keepdims=True)
    rest = jnp.where(lane == i1, neg, logits)
    m2 = jnp.max(rest, axis=-1, keepdims=True)
    i2 = jnp.min(jnp.where(rest == m2, lane, LANES), axis=-1, keepdims=True)
    e2 = jnp.exp(m2 - m1)
    den = 1.0 + e2
    gate_ref[...] = jnp.where(lane == 0, 1.0 / den, jnp.where(lane == 1, e2 / den, 0.0))
    chosen = jnp.where((lane == i1) | (lane == i2), 1.0, 0.0)
    before = jnp.dot(tri_ref[...], chosen.astype(BF16), preferred_element_type=F32) + run_ref[...]
    r1 = jnp.sum(jnp.where(lane == i1, before, 0.0), axis=-1, keepdims=True).astype(jnp.int32)
    r2 = jnp.sum(jnp.where(lane == i2, before, 0.0), axis=-1, keepdims=True).astype(jnp.int32)
    idx_ref[...] = jnp.where(lane == 0, i1, jnp.where(lane == 1, i2, jnp.where(lane == 2, r1, jnp.where(lane == 3, r2, 0))))
    run = run_ref[...] + jnp.sum(chosen, axis=0, keepdims=True)
    run_ref[...] = run
    cnt_ref[...] = run.astype(jnp.int32)


def _router(x, w_r, b_r):
    t, d = x.shape
    n_experts = w_r.shape[1]
    w_pad = jnp.zeros((d, LANES), F32).at[:, :n_experts].set(w_r)
    w_hi = w_pad.astype(BF16)
    w_split = jnp.concatenate([w_hi, (w_pad - w_hi.astype(F32)).astype(BF16)], axis=1)
    b_pad = jnp.zeros((1, LANES), F32).at[0, :n_experts].set(b_r)
    tm = _tile(t, 1024)
    tri = (jnp.arange(tm)[:, None] > jnp.arange(tm)[None, :]).astype(BF16)
    row = lambda i: (i, 0)
    fixed = lambda i: (0, 0)
    return pl.pallas_call(
        functools.partial(_router_kernel, n_experts=n_experts),
        grid=(t // tm,),
        in_specs=[pl.BlockSpec((tm, d), row), pl.BlockSpec((d, 2 * LANES), fixed), pl.BlockSpec((1, LANES), fixed),
                  pl.BlockSpec((tm, tm), fixed)],
        out_specs=[pl.BlockSpec((tm, LANES), row), pl.BlockSpec((tm, LANES), row), pl.BlockSpec((1, LANES), fixed)],
        out_shape=[jax.ShapeDtypeStruct((t, LANES), jnp.int32), jax.ShapeDtypeStruct((t, LANES), F32),
                   jax.ShapeDtypeStruct((1, LANES), jnp.int32)],
        scratch_shapes=[pltpu.VMEM((1, LANES), F32)],
        compiler_params=_params("arbitrary"),
        name="router",
    )(x, w_split, b_pad, tri)


def _moe_dispatch(idx, counts, n_experts, tm):
    t = idx.shape[0]
    s = t * TOP_K
    padded = ((counts + tm - 1) // tm) * tm
    pend = jnp.cumsum(padded)
    pstart = pend - padded
    onehot = idx[:, :TOP_K, None] == jnp.arange(n_experts, dtype=jnp.int32)
    dest = jnp.sum(jnp.where(onehot, pstart, 0), axis=-1) + idx[:, TOP_K:2 * TOP_K]
    n_tiles = s // tm + n_experts
    tok = jnp.broadcast_to(jnp.arange(t, dtype=jnp.int32)[:, None], (t, TOP_K))
    src_tok = jnp.zeros((n_tiles * tm,), jnp.int32).at[dest.reshape(s)].set(
        tok.reshape(s), unique_indices=True, mode="promise_in_bounds")
    nv = (pend[-1] // tm).astype(jnp.int32)
    tile_idx = jnp.arange(n_tiles, dtype=jnp.int32)
    te = jnp.searchsorted(pend, jnp.minimum(tile_idx, nv - 1) * tm, side="right").astype(jnp.int32)
    return src_tok, te, nv.reshape(1), dest


def _moe_layer(x, xb, w_r, b_r, w1, w3, w2, j, ln_g, ln_b, alpha):
    t, d = x.shape
    n_experts = w_r.shape[1]
    tm = _tile(t * TOP_K, 512)
    idx, gate, counts = _router(x, w_r, b_r)
    src_tok, te, nv, dest = _moe_dispatch(idx, counts[0, :n_experts], n_experts, tm)
    rows = lambda a, r: a.at[r].get(mode="promise_in_bounds")
    xs = rows(xb, src_tok)
    out_p = _ffn_moe(xs, te + j * n_experts, nv, w1, w3, w2, tm)
    o0 = rows(out_p, dest[:, 0])
    o1 = rows(out_p, dest[:, 1])
    return _add_res_ln(x, o0, o1, gate, ln_g, ln_b, alpha)


def kernel(x_prompt, x_sample, ln_mix_g, ln_mix_b, ln_ffn_g, ln_ffn_b, a_w_in, a_b_in, a_ln_g, a_ln_b, a_w_s, a_b_s, a_w_out, b_w_qkv, b_q_g, b_k_g, b_w_o, c_w, c_b, c_scale, d_w_in, d_b_in, d_w_dw, d_b_dw, d_ln_g, d_ln_b, d_w_out, d_b_out, f_w1, f_w3, f_w2, m_w_r, m_b_r, m_w1, m_w3, m_w2):
    d = x_prompt.shape[-1]
    depth = ln_mix_g.shape[0]
    alpha = (2.0 * depth) ** 0.25
    vec = lambda v: v.reshape(1, -1).astype(F32)
    two_heads = lambda g: jnp.tile(g.astype(F32), 2).reshape(1, LANES)

    a_w_in, a_w_out, b_w_qkv, b_w_o, d_w_in, d_w_out = map(_to_bf16, (a_w_in, a_w_out, b_w_qkv, b_w_o, d_w_in, d_w_out))
    f_w1, f_w3, f_w2 = map(_to_bf16, (f_w1, f_w3, f_w2))
    n_moe, n_experts = m_w1.shape[:2]
    m_w1, m_w3, m_w2 = (_to_bf16(w).reshape((n_moe * n_experts,) + w.shape[2:]) for w in (m_w1, m_w3, m_w2))
    a_w_s, c_w = a_w_s.astype(BF16), c_w.astype(BF16)
    d_w_dw = jnp.repeat(d_w_dw.astype(F32), SUBLANES_F32, axis=1)

    groups = []
    for xin in (x_prompt, x_sample):
        bsz, n, _ = xin.shape
        x = xin.reshape(bsz * n, d)
        groups.append([x, _to_bf16(x), n])
    for l in range(depth):
        kind, i = l % 4, l // 4
        mg, mb = vec(ln_mix_g[l]), vec(ln_mix_b[l])
        fg, fb = vec(ln_ffn_g[l]), vec(ln_ffn_b[l])
        j = l // 2
        for grp in groups:
            x, xb, n = grp
            if kind == 0:
                z = _mm_gelu(xb, a_w_in[i], vec(a_b_in[i]))
                x, xb = _gmlp_gate(z, x, vec(a_ln_g[i]), vec(a_ln_b[i]), a_w_s[i], a_b_s[i].T.astype(F32),
                                   a_w_out[i], mg, mb, alpha)
            elif kind == 1:
                q, kt, va = _qkv(xb, b_w_qkv[i], two_heads(b_q_g[i]), two_heads(b_k_g[i]), n)
                o = _attention(q, kt, va, n, d)
                x, xb = _mm_res_ln(o, b_w_o[i], x, mg, mb, alpha)
            elif kind == 2:
                x, xb = _pool_mixer(x, c_w[i], vec(c_b[i]), vec(c_scale[i]), mg, mb, alpha, n)
            else:
                hg = _mm_glu(xb, d_w_in[i], vec(d_b_in[i]))
                x, xb = _conv_mixer(hg, x, d_w_dw[i], vec(d_b_dw[i]), vec(d_ln_g[i]), vec(d_ln_b[i]),
                                    d_w_out[i], vec(d_b_out[i]), mg, mb, alpha, n)
            if l % 2 == 0:
                x, xb = _ffn_dense(xb, x, f_w1, f_w3, f_w2, j, fg, fb, alpha)
            else:
                x, xb = _moe_layer(x, xb, m_w_r[j], m_b_r[j], m_w1, m_w3, m_w2, j, fg, fb, alpha)
            grp[0], grp[1] = x, xb
    return tuple(x.reshape(xin.shape) for (x, _, _), xin in zip(groups, (x_prompt, x_sample)))
```
